```python
import math
import jax, jax.numpy as jnp
from jax import lax
import numpy as np

D_MODEL = 4096
BATCH = 1
SEQ = 8192
DEPTH = 1

HEAD_DIM = 128
N_HEADS = D_MODEL // (2 * HEAD_DIM)
N_KV_HEADS = 4
GROUP = N_HEADS // N_KV_HEADS
ATTN_WIDTH = N_HEADS * HEAD_DIM
KV_WIDTH = N_KV_HEADS * HEAD_DIM
WINDOW = 128
BLOCK = 128
POOL_WIDTH = D_MODEL - ATTN_WIDTH
POOL_WINDOWS = (2, 4, 8, 16)
N_POOL_GROUPS = len(POOL_WINDOWS)
POOL_GROUP_W = POOL_WIDTH // N_POOL_GROUPS
MIX_WIDTH = ATTN_WIDTH + POOL_WIDTH
IN_COLS = ATTN_WIDTH + 2 * KV_WIDTH + POOL_WIDTH
D_FF = int(math.ceil(8 * D_MODEL / 3 / 256)) * 256
RMS_EPS = 1e-6
NEG_INF = -1e30

kernel_name = "hybrid_swa_alibi_multiscale_pool_block"


def _rmsnorm(x, g):
    xf = x.astype(jnp.float32)
    y = xf * lax.rsqrt(jnp.mean(xf * xf, axis=-1, keepdims=True) + RMS_EPS)
    return (y * g.astype(jnp.float32)).astype(x.dtype)


def _alibi_slopes(n):
    return 2.0 ** (-8.0 * jnp.arange(1, n + 1, dtype=jnp.float32) / n)


def _banded_window_attention(q, k, v, sink_logits):
    B, S, H, D = q.shape
    n = S // BLOCK
    pad = ((0, 0), (BLOCK, BLOCK), (0, 0), (0, 0))
    kp = jnp.pad(k, pad).reshape(B, n + 2, BLOCK, N_KV_HEADS, D)
    vp = jnp.pad(v, pad).reshape(B, n + 2, BLOCK, N_KV_HEADS, D)
    kb = jnp.concatenate([kp[:, :-2], kp[:, 1:-1], kp[:, 2:]], axis=2)
    vb = jnp.concatenate([vp[:, :-2], vp[:, 1:-1], vp[:, 2:]], axis=2)
    qb = q.reshape(B, n, BLOCK, N_KV_HEADS, GROUP, D)
    s = jnp.einsum('bnqkgd,bnskd->bnkgqs', qb, kb,
                   preferred_element_type=jnp.float32)
    qi = jnp.arange(BLOCK)[:, None]
    kj = jnp.arange(3 * BLOCK)[None, :]
    dist = kj - BLOCK - qi
    kpos = jnp.arange(n)[:, None] * BLOCK - BLOCK + jnp.arange(3 * BLOCK)[None, :]
    valid = (jnp.abs(dist) <= WINDOW)[None] & ((kpos >= 0) & (kpos < S))[:, None, :]
    slopes = _alibi_slopes(N_HEADS).reshape(N_KV_HEADS, GROUP, 1, 1)
    bias = -slopes * jnp.abs(dist).astype(jnp.float32)
    logits = jnp.where(valid[None, :, None, None], s + bias[None, None], NEG_INF)
    sink = sink_logits.astype(jnp.float32).reshape(1, 1, N_KV_HEADS, GROUP, 1, 1)
    lse = jnp.logaddexp(jax.nn.logsumexp(logits, axis=-1, keepdims=True), sink)
    p = jnp.exp(logits - lse)
    o = jnp.einsum('bnkgqs,bnskd->bnqkgd', p.astype(v.dtype), vb)
    return o.reshape(B, S, H * D)


def _multiscale_pool(p, pool_w, pool_scale):
    B, S, _ = p.shape
    pf = p.astype(jnp.float32).reshape(B, S, N_POOL_GROUPS, POOL_GROUP_W)
    cs = jnp.concatenate([jnp.zeros((B, 1, N_POOL_GROUPS, POOL_GROUP_W), jnp.float32),
                          jnp.cumsum(pf, axis=1)], axis=1)
    t = jnp.arange(S)
    outs = []
    for gi, w in enumerate(POOL_WINDOWS):
        left = w // 2
        right = w - 1 - left
        lo = jnp.clip(t - left, 0, S)
        hi = jnp.clip(t + right + 1, 0, S)
        seg = cs[:, :, gi]
        win_sum = jnp.take(seg, hi, axis=1) - jnp.take(seg, lo, axis=1)
        cnt = (hi - lo).astype(jnp.float32)[None, :, None]
        outs.append(win_sum / cnt - pf[:, :, gi])
    u = jnp.stack(outs, axis=2).astype(p.dtype)
    y = jnp.einsum('bsgc,gcd->bsgd', u, pool_w).reshape(B, S, POOL_WIDTH)
    return y * pool_scale


def setup_inputs(seed: int = 0) -> dict:
    key = jax.random.key(seed)
    ks = jax.random.split(key, 13)
    f32 = jnp.float32
    nrm = lambda k, shape, fan_in: jax.random.normal(k, shape, f32) * (fan_in ** -0.5)
    return {
        "x": jax.random.normal(ks[0], (BATCH, SEQ, D_MODEL), f32),
        "norm1_g": 1.0 + 0.02 * jax.random.normal(ks[1], (D_MODEL,), f32),
        "w_in": nrm(ks[2], (D_MODEL, IN_COLS), D_MODEL),
        "q_norm_g": 1.0 + 0.02 * jax.random.normal(ks[3], (HEAD_DIM,), f32),
        "k_norm_g": 1.0 + 0.02 * jax.random.normal(ks[4], (HEAD_DIM,), f32),
        "sink_logits": 0.5 * jax.random.normal(ks[5], (N_HEADS,), f32),
        "pool_w": nrm(ks[6], (N_POOL_GROUPS, POOL_GROUP_W, POOL_GROUP_W), POOL_GROUP_W),
        "pool_scale": 1.0 + 0.02 * jax.random.normal(ks[7], (POOL_WIDTH,), f32),
        "w_out": nrm(ks[8], (MIX_WIDTH, D_MODEL), MIX_WIDTH),
        "norm2_g": 1.0 + 0.02 * jax.random.normal(ks[9], (D_MODEL,), f32),
        "w_gate": nrm(ks[10], (D_MODEL, D_FF), D_MODEL),
        "w_up": nrm(ks[11], (D_MODEL, D_FF), D_MODEL),
        "w_down": nrm(ks[12], (D_FF, D_MODEL), D_FF),
    }


def reference(x, norm1_g, w_in, q_norm_g, k_norm_g, sink_logits, pool_w, pool_scale,
              w_out, norm2_g, w_gate, w_up, w_down):
    B, S, _ = x.shape
    for _layer in range(DEPTH):
        h = _rmsnorm(x, norm1_g)
        proj = h @ w_in
        q = proj[..., :ATTN_WIDTH].reshape(B, S, N_HEADS, HEAD_DIM)
        k = proj[..., ATTN_WIDTH:ATTN_WIDTH + KV_WIDTH].reshape(B, S, N_KV_HEADS, HEAD_DIM)
        v = proj[..., ATTN_WIDTH + KV_WIDTH:ATTN_WIDTH + 2 * KV_WIDTH].reshape(B, S, N_KV_HEADS, HEAD_DIM)
        p = proj[..., ATTN_WIDTH + 2 * KV_WIDTH:]
        q = _rmsnorm(q, q_norm_g) * jnp.asarray(HEAD_DIM ** -0.5, q.dtype)
        k = _rmsnorm(k, k_norm_g)
        attn_out = _banded_window_attention(q, k, v, sink_logits)
        pool_out = _multiscale_pool(p, pool_w, pool_scale)
        mixed = jnp.concatenate([attn_out, pool_out], axis=-1)
        x = x + mixed @ w_out
        h2 = _rmsnorm(x, norm2_g)
        x = x + (jax.nn.silu(h2 @ w_gate) * (h2 @ w_up)) @ w_down
    return x
```

```python
import functools

import jax
import jax.numpy as jnp
from jax import lax
from jax.experimental import pallas as pl
from jax.experimental.pallas import tpu as pltpu

F32 = jnp.float32
BF16 = jnp.bfloat16

D_MODEL = 4096
SEQ = 8192
HEAD_DIM = 128
N_HEADS = 16
N_KV_HEADS = 4
GROUP = N_HEADS // N_KV_HEADS
ATTN_WIDTH = N_HEADS * HEAD_DIM
KV_WIDTH = N_KV_HEADS * HEAD_DIM
POOL_WIDTH = D_MODEL - ATTN_WIDTH
POOL_WINDOWS = (2, 4, 8, 16)
N_POOL_GROUPS = len(POOL_WINDOWS)
POOL_GROUP_W = POOL_WIDTH // N_POOL_GROUPS
IN_COLS = ATTN_WIDTH + 2 * KV_WIDTH + POOL_WIDTH
D_FF = 11008
WINDOW = 128
BLOCK = 128
RMS_EPS = 1e-6
NEG_INF = -1e30
Q_SCALE = HEAD_DIM ** -0.5
ALIBI_SLOPES = tuple(2.0 ** (-8.0 * h / N_HEADS) for h in range(1, N_HEADS + 1))

VMEM_LIMIT_BYTES = 56 * 1024 * 1024
POOL_HALO = 16


def _params(*semantics):
    return pltpu.CompilerParams(dimension_semantics=semantics, vmem_limit_bytes=VMEM_LIMIT_BYTES)


INPROJ_BM = 512
INPROJ_BN = 1024
_Q_BLOCKS = ATTN_WIDTH // INPROJ_BN
_KV_BLOCK = _Q_BLOCKS


def _head_rmsnorm(blk, gain, scale):
    ms = jnp.mean(blk * blk, axis=-1, keepdims=True)
    y = blk * lax.rsqrt(ms + RMS_EPS) * gain
    return y if scale is None else y * scale


def _inproj_kernel(x_ref, g1_ref, w_ref, qg_ref, kg_ref, o_ref, h_ref):
    j = pl.program_id(1)

    @pl.when(j == 0)
    def _():
        x = x_ref[...]
        ms = jnp.mean(x * x, axis=-1, keepdims=True)
        h_ref[...] = (x * lax.rsqrt(ms + RMS_EPS) * g1_ref[...]).astype(BF16)

    acc = jnp.dot(h_ref[...], w_ref[...], preferred_element_type=F32)

    @pl.when(j < _Q_BLOCKS)
    def _():
        for hd in range(INPROJ_BN // HEAD_DIM):
            sl = slice(hd * HEAD_DIM, (hd + 1) * HEAD_DIM)
            o_ref[:, sl] = _head_rmsnorm(acc[:, sl], qg_ref[...], Q_SCALE).astype(BF16)

    @pl.when(j == _KV_BLOCK)
    def _():
        for hd in range(N_KV_HEADS):
            sl = slice(hd * HEAD_DIM, (hd + 1) * HEAD_DIM)
            o_ref[:, sl] = _head_rmsnorm(acc[:, sl], kg_ref[...], None).astype(BF16)
        o_ref[:, KV_WIDTH:] = acc[:, KV_WIDTH:].astype(BF16)

    @pl.when(j > _KV_BLOCK)
    def _():
        o_ref[...] = acc.astype(BF16)


def _inproj(x, g1, w_in, qg, kg):
    grid = (SEQ // INPROJ_BM, IN_COLS // INPROJ_BN)
    return pl.pallas_call(
        _inproj_kernel,
        name="inproj",
        grid=grid,
        in_specs=[
            pl.BlockSpec((INPROJ_BM, D_MODEL), lambda i, j: (i, 0)),
            pl.BlockSpec((1, D_MODEL), lambda i, j: (0, 0)),
            pl.BlockSpec((D_MODEL, INPROJ_BN), lambda i, j: (0, j)),
            pl.BlockSpec((1, HEAD_DIM), lambda i, j: (0, 0)),
            pl.BlockSpec((1, HEAD_DIM), lambda i, j: (0, 0)),
        ],
        out_specs=pl.BlockSpec((INPROJ_BM, INPROJ_BN), lambda i, j: (i, j)),
        out_shape=jax.ShapeDtypeStruct((SEQ, IN_COLS), BF16),
        scratch_shapes=[pltpu.VMEM((INPROJ_BM, D_MODEL), BF16)],
        compiler_params=_params("arbitrary", "arbitrary"),
    )(x, g1, w_in, qg, kg)


N_QBLOCKS = SEQ // BLOCK
KEYS = 3 * BLOCK
_K_COLBLOCK = ATTN_WIDTH // KV_WIDTH
_V_COLBLOCK = _K_COLBLOCK + 1


def _attn_kernel(sink_ref, q_ref, kp_ref, kc_ref, kn_ref, vp_ref, vc_ref, vn_ref, o_ref, bias_ref):
    n = pl.program_id(0)

    @pl.when(n == 0)
    def _():
        qi = lax.broadcasted_iota(jnp.int32, (BLOCK, KEYS), 0)
        kj = lax.broadcasted_iota(jnp.int32, (BLOCK, KEYS), 1)
        dist = jnp.abs(kj - BLOCK - qi)
        in_band = dist <= WINDOW
        distf = dist.astype(F32)
        for h in range(N_HEADS):
            bias_ref[h] = jnp.where(in_band, -ALIBI_SLOPES[h] * distf, NEG_INF)

    kpos = n * BLOCK - BLOCK + lax.broadcasted_iota(jnp.int32, (1, KEYS), 1)
    edge = jnp.where((kpos >= 0) & (kpos < SEQ), 0.0, NEG_INF).astype(F32)
    row_head = lax.broadcasted_iota(jnp.int32, (GROUP * BLOCK, 1), 0) // BLOCK

    for kh in range(N_KV_HEADS):
        ksl = slice(kh * HEAD_DIM, (kh + 1) * HEAD_DIM)
        kw = jnp.concatenate([kp_ref[:, ksl], kc_ref[:, ksl], kn_ref[:, ksl]], axis=0)
        vw = jnp.concatenate([vp_ref[:, ksl], vc_ref[:, ksl], vn_ref[:, ksl]], axis=0)
        h0 = kh * GROUP
        qs = jnp.concatenate(
            [q_ref[:, (h0 + g) * HEAD_DIM:(h0 + g + 1) * HEAD_DIM] for g in range(GROUP)], axis=0)
        s = lax.dot_general(qs, kw, (((1,), (1,)), ((), ())), preferred_element_type=F32)
        bias = bias_ref[h0:h0 + GROUP].reshape(GROUP * BLOCK, KEYS)
        logits = s + bias + edge
        sink = jnp.full((GROUP * BLOCK, 1), sink_ref[h0 + GROUP - 1], F32)
        for g in range(GROUP - 1):
            sink = jnp.where(row_head == g, sink_ref[h0 + g], sink)
        m = jnp.maximum(jnp.max(logits, axis=-1, keepdims=True), sink)
        e = jnp.exp(logits - m)
        den = jnp.sum(e, axis=-1, keepdims=True) + jnp.exp(sink - m)
        o = jnp.dot(e.astype(BF16), vw, preferred_element_type=F32) * (1.0 / den)
        for g in range(GROUP):
            o_ref[:, (h0 + g) * HEAD_DIM:(h0 + g + 1) * HEAD_DIM] = (
                o[g * BLOCK:(g + 1) * BLOCK].astype(BF16))


def _attention(sink_logits, proj):
    last = N_QBLOCKS - 1
    prev_i = lambda n: jnp.maximum(n - 1, 0)
    next_i = lambda n: jnp.minimum(n + 1, last)
    kv_spec = lambda row_fn, col: pl.BlockSpec((BLOCK, KV_WIDTH), lambda n: (row_fn(n), col))
    same = lambda n: n
    return pl.pallas_call(
        _attn_kernel,
        name="attn",
        grid=(N_QBLOCKS,),
        in_specs=[
            pl.BlockSpec(memory_space=pltpu.SMEM),
            pl.BlockSpec((BLOCK, ATTN_WIDTH), lambda n: (n, 0)),
            kv_spec(prev_i, _K_COLBLOCK), kv_spec(same, _K_COLBLOCK), kv_spec(next_i, _K_COLBLOCK),
            kv_spec(prev_i, _V_COLBLOCK), kv_spec(same, _V_COLBLOCK), kv_spec(next_i, _V_COLBLOCK),
        ],
        out_specs=pl.BlockSpec((BLOCK, ATTN_WIDTH), lambda n: (n, 0)),
        out_shape=jax.ShapeDtypeStruct((SEQ, ATTN_WIDTH), BF16),
        scratch_shapes=[pltpu.VMEM((N_HEADS, BLOCK, KEYS), F32)],
        compiler_params=_params("arbitrary"),
    )(sink_logits, proj, proj, proj, proj, proj, proj, proj)


POOL_BM = 512
_P_COLBLOCK = (ATTN_WIDTH + 2 * KV_WIDTH) // POOL_GROUP_W


def _pool_kernel(pp_ref, pc_ref, pn_ref, w_ref, sc_ref, o_ref):
    gi = pl.program_id(0)
    i = pl.program_id(1)
    ext = POOL_BM + 2 * POOL_HALO
    left = jnp.left_shift(1, gi)
    right = left - 1
    t = lax.broadcasted_iota(jnp.int32, (POOL_BM, ext), 0)
    s = lax.broadcasted_iota(jnp.int32, (POOL_BM, ext), 1) - POOL_HALO
    s_glob = i * POOL_BM + s
    member = (s >= t - left) & (s <= t + right) & (s_glob >= 0) & (s_glob < SEQ)
    band = jnp.where(member, 1.0, 0.0).astype(BF16)
    p_cur = pc_ref[...]
    p_ext = jnp.concatenate([pp_ref[...], p_cur, pn_ref[...]], axis=0)
    win_sum = jnp.dot(band, p_ext, preferred_element_type=F32)
    tg = i * POOL_BM + lax.broadcasted_iota(jnp.int32, (POOL_BM, 1), 0)
    cnt = jnp.minimum(tg + right + 1, SEQ) - jnp.maximum(tg - left, 0)
    u = win_sum * (1.0 / cnt.astype(F32)) - p_cur.astype(F32)
    y = jnp.dot(u.astype(BF16), w_ref[0], preferred_element_type=F32)
    o_ref[...] = (y * sc_ref[...]).astype(BF16)


def _pool(proj, pool_w, pool_scale):
    rows_per_halo = POOL_BM // POOL_HALO
    n_halo_blocks = SEQ // POOL_HALO
    return pl.pallas_call(
        _pool_kernel,
        name="pool",
        grid=(N_POOL_GROUPS, SEQ // POOL_BM),
        in_specs=[
            pl.BlockSpec((POOL_HALO, POOL_GROUP_W),
                         lambda g, i: (jnp.maximum(i * rows_per_halo - 1, 0), _P_COLBLOCK + g)),
            pl.BlockSpec((POOL_BM, POOL_GROUP_W), lambda g, i: (i, _P_COLBLOCK + g)),
            pl.BlockSpec((POOL_HALO, POOL_GROUP_W),
                         lambda g, i: (jnp.minimum((i + 1) * rows_per_halo, n_halo_blocks - 1),
                                       _P_COLBLOCK + g)),
            pl.BlockSpec((1, POOL_GROUP_W, POOL_GROUP_W), lambda g, i: (g, 0, 0)),
            pl.BlockSpec((1, POOL_GROUP_W), lambda g, i: (0, g)),
        ],
        out_specs=pl.BlockSpec((POOL_BM, POOL_GROUP_W), lambda g, i: (i, g)),
        out_shape=jax.ShapeDtypeStruct((SEQ, POOL_WIDTH), BF16),
        compiler_params=_params("arbitrary", "arbitrary"),
    )(proj, proj, proj, pool_w, pool_scale)


OUTPROJ_BM = 512
OUTPROJ_BN = 1024
_OUTPROJ_NJ = D_MODEL // OUTPROJ_BN


def _outproj_kernel(at_ref, po_ref, w_ref, x_ref, g2_ref, x1_ref, h2_ref, row_ref, ss_ref):
    j = pl.program_id(1)
    acc = jnp.dot(at_ref[...], w_ref[:ATTN_WIDTH, :], preferred_element_type=F32)
    acc = acc + jnp.dot(po_ref[...], w_ref[ATTN_WIDTH:, :], preferred_element_type=F32)
    x1 = x_ref[...] + acc
    x1_ref[...] = x1
    row_ref[j] = x1
    ssq = jnp.sum(x1 * x1, axis=-1, keepdims=True)

    @pl.when(j == 0)
    def _():
        ss_ref[...] = ssq

    @pl.when(j > 0)
    def _():
        ss_ref[...] += ssq

    @pl.when(j == _OUTPROJ_NJ - 1)
    def _():
        inv = lax.rsqrt(ss_ref[...] * (1.0 / D_MODEL) + RMS_EPS)
        for jj in range(_OUTPROJ_NJ):
            sl = slice(jj * OUTPROJ_BN, (jj + 1) * OUTPROJ_BN)
            h2_ref[:, sl] = (row_ref[jj] * inv * g2_ref[:, sl]).astype(BF16)


def _outproj(attn, pool, w_out, x, g2):
    return pl.pallas_call(
        _outproj_kernel,
        name="outproj",
        grid=(SEQ // OUTPROJ_BM, _OUTPROJ_NJ),
        in_specs=[
            pl.BlockSpec((OUTPROJ_BM, ATTN_WIDTH), lambda i, j: (i, 0)),
            pl.BlockSpec((OUTPROJ_BM, POOL_WIDTH), lambda i, j: (i, 0)),
            pl.BlockSpec((D_MODEL, OUTPROJ_BN), lambda i, j: (0, j)),
            pl.BlockSpec((OUTPROJ_BM, OUTPROJ_BN), lambda i, j: (i, j)),
            pl.BlockSpec((1, D_MODEL), lambda i, j: (0, 0)),
        ],
        out_specs=[
            pl.BlockSpec((OUTPROJ_BM, OUTPROJ_BN), lambda i, j: (i, j)),
            pl.BlockSpec((OUTPROJ_BM, D_MODEL), lambda i, j: (i, 0)),
        ],
        out_shape=[
            jax.ShapeDtypeStruct((SEQ, D_MODEL), F32),
            jax.ShapeDtypeStruct((SEQ, D_MODEL), BF16),
        ],
        scratch_shapes=[
            pltpu.VMEM((_OUTPROJ_NJ, OUTPROJ_BM, OUTPROJ_BN), F32),
            pltpu.VMEM((OUTPROJ_BM, 1), F32),
        ],
        compiler_params=_params("arbitrary", "arbitrary"),
    )(attn, pool, w_out, x, g2)


FFN_BM = 1024
FFN_BF = 256
DOWN_BN = 512
DOWN_BK = D_FF // 2


def _ffn_up_kernel(h_ref, wg_ref, wu_ref, a_ref):
    h = h_ref[...]
    g = jnp.dot(h, wg_ref[...], preferred_element_type=F32)
    u = jnp.dot(h, wu_ref[...], preferred_element_type=F32)
    a_ref[...] = (g * jax.nn.sigmoid(g) * u).astype(BF16)


def _ffn_up(h2, w_gate, w_up):
    return pl.pallas_call(
        _ffn_up_kernel,
        name="ffn_up",
        grid=(SEQ // FFN_BM, D_FF // FFN_BF),
        in_specs=[
            pl.BlockSpec((FFN_BM, D_MODEL), lambda i, f: (i, 0)),
            pl.BlockSpec((D_MODEL, FFN_BF), lambda i, f: (0, f)),
            pl.BlockSpec((D_MODEL, FFN_BF), lambda i, f: (0, f)),
        ],
        out_specs=pl.BlockSpec((FFN_BM, FFN_BF), lambda i, f: (i, f)),
        out_shape=jax.ShapeDtypeStruct((SEQ, D_FF), BF16),
        compiler_params=_params("arbitrary", "arbitrary"),
    )(h2, w_gate, w_up)


def _ffn_down_kernel(a_ref, w_ref, x1_ref, o_ref):
    k = pl.program_id(2)
    part = jnp.dot(a_ref[...], w_ref[...], preferred_element_type=F32)

    @pl.when(k == 0)
    def _():
        o_ref[...] = x1_ref[...] + part

    @pl.when(k > 0)
    def _():
        o_ref[...] += part


def _ffn_down(a, w_down, x1):
    return pl.pallas_call(
        _ffn_down_kernel,
        name="ffn_down",
        grid=(SEQ // FFN_BM, D_MODEL // DOWN_BN, D_FF // DOWN_BK),
        in_specs=[
            pl.BlockSpec((FFN_BM, DOWN_BK), lambda i, j, k: (i, k)),
            pl.BlockSpec((DOWN_BK, DOWN_BN), lambda i, j, k: (k, j)),
            pl.BlockSpec((FFN_BM, DOWN_BN), lambda i, j, k: (i, j)),
        ],
        out_specs=pl.BlockSpec((FFN_BM, DOWN_BN), lambda i, j, k: (i, j)),
        out_shape=jax.ShapeDtypeStruct((SEQ, D_MODEL), F32),
        compiler_params=_params("arbitrary", "arbitrary", "arbitrary"),
    )(a, w_down, x1)


def kernel(x, norm1_g, w_in, q_norm_g, k_norm_g, sink_logits, pool_w, pool_scale, w_out, norm2_g,
           w_gate, w_up, w_down):
    b, s, d = x.shape
    x2 = x.reshape(b * s, d)
    proj = _inproj(x2, norm1_g.reshape(1, D_MODEL), w_in.astype(BF16),
                   q_norm_g.reshape(1, HEAD_DIM), k_norm_g.reshape(1, HEAD_DIM))
    attn = _attention(sink_logits, proj)
    pool = _pool(proj, pool_w.astype(BF16), pool_scale.reshape(1, POOL_WIDTH))
    x1, h2 = _outproj(attn, pool, w_out.astype(BF16), x2, norm2_g.reshape(1, D_MODEL))
    a = _ffn_up(h2, w_gate.astype(BF16), w_up.astype(BF16))
    out = _ffn_down(a, w_down.astype(BF16), x1)
    return out.reshape(b, s, d)
```

```python
import jax
import jax.numpy as jnp
from jax import lax
from jax.experimental import pallas as pl
from jax.experimental.pallas import tpu as pltpu

F32 = jnp.float32
BF16 = jnp.bfloat16

D_MODEL = 4096
SEQ = 8192
HEAD_DIM = 128
N_HEADS = 16
N_KV_HEADS = 4
GROUP = N_HEADS // N_KV_HEADS
ATTN_WIDTH = N_HEADS * HEAD_DIM
KV_WIDTH = N_KV_HEADS * HEAD_DIM
POOL_WIDTH = D_MODEL - ATTN_WIDTH
POOL_WINDOWS = (2, 4, 8, 16)
N_POOL_GROUPS = len(POOL_WINDOWS)
POOL_GROUP_W = POOL_WIDTH // N_POOL_GROUPS
IN_COLS = ATTN_WIDTH + 2 * KV_WIDTH + POOL_WIDTH
D_FF = 11008
WINDOW = 128
BLOCK = 128
RMS_EPS = 1e-6
NEG_INF = -1e30
Q_SCALE = HEAD_DIM ** -0.5
LOG2E = 1.4426950408889634
ALIBI_SLOPES = tuple(2.0 ** (-8.0 * h / N_HEADS) for h in range(1, N_HEADS + 1))

VMEM_LIMIT_BYTES = 56 * 1024 * 1024
MXU_N = 256
LANES = 128
POOL_HALO = 16


def _params(*semantics):
    return pltpu.CompilerParams(dimension_semantics=semantics, vmem_limit_bytes=VMEM_LIMIT_BYTES)


INPROJ_BM = 1024
INPROJ_BN = 1024
_IN_NI = SEQ // INPROJ_BM
_IN_NJ = IN_COLS // INPROJ_BN
_IN_SUBS = 4
_IN_SUB_ROWS = INPROJ_BM // _IN_SUBS
_IN_MROWS = 512
_N_NORM_HEADS = N_HEADS + N_KV_HEADS
WO_SLAB = 128
_WO_SLABS = D_MODEL // WO_SLAB


def _inproj_kernel(x_ref, g1_ref, w_ref, gain_ref, wo_ref, o_ref, wo16_ref,
                   h_a, rs_a, h_b, rs_b):
    i = pl.program_id(0)
    j = pl.program_id(1)
    row0 = pl.multiple_of(jnp.minimum(j, _IN_SUBS - 1) * _IN_SUB_ROWS, _IN_SUB_ROWS)

    def stage_rows(h_ref, rs_ref):
        x = x_ref[...]
        ms = jnp.mean(x * x, axis=-1, keepdims=True)
        rs_ref[pl.ds(row0, _IN_SUB_ROWS), :] = lax.rsqrt(ms + RMS_EPS)
        h_ref[pl.ds(row0, _IN_SUB_ROWS), :] = (x * g1_ref[...]).astype(BF16)

    def project(h_ref, rs_ref):
        for c in range(INPROJ_BN // MXU_N):
            for r in range(INPROJ_BM // _IN_MROWS):
                rows = slice(r * _IN_MROWS, (r + 1) * _IN_MROWS)
                acc = jnp.dot(h_ref[rows, :], w_ref[:, c * MXU_N:(c + 1) * MXU_N],
                              preferred_element_type=F32) * rs_ref[rows, :]
                for hh in range(MXU_N // HEAD_DIM):
                    col = c * MXU_N + hh * HEAD_DIM
                    head = j * (INPROJ_BN // HEAD_DIM) + col // HEAD_DIM
                    blk = acc[:, hh * HEAD_DIM:(hh + 1) * HEAD_DIM]
                    ms = jnp.mean(blk * blk, axis=-1, keepdims=True)
                    scale = jnp.where(head < _N_NORM_HEADS, lax.rsqrt(ms + RMS_EPS), 1.0)
                    o_ref[rows, col:col + HEAD_DIM] = (
                        blk * scale * gain_ref[:, col:col + HEAD_DIM]).astype(BF16)

    @pl.when(i == 0)
    def _():
        stage_rows(h_a, rs_a)

    @pl.when((i > 0) & (i % 2 == 1))
    def _():
        stage_rows(h_b, rs_b)
        project(h_a, rs_a)

    @pl.when((i > 0) & (i % 2 == 0))
    def _():
        stage_rows(h_a, rs_a)
        project(h_b, rs_b)

    @pl.when(i * _IN_NJ + j < _WO_SLABS)
    def _():
        wo16_ref[...] = wo_ref[...].astype(BF16)


def _inproj(x, g1, w_in16, gain, w_out):
    assert _IN_SUBS <= _IN_NJ and _WO_SLABS <= (_IN_NI + 1) * _IN_NJ
    x_index = lambda i, j: (jnp.minimum(i, _IN_NI - 1) * _IN_SUBS + jnp.minimum(j, _IN_SUBS - 1), 0)
    wo_index = lambda i, j: (jnp.minimum(i * _IN_NJ + j, _WO_SLABS - 1), 0)
    return pl.pallas_call(
        _inproj_kernel,
        name="inproj",
        grid=(_IN_NI + 1, _IN_NJ),
        in_specs=[
            pl.BlockSpec((_IN_SUB_ROWS, D_MODEL), x_index),
            pl.BlockSpec((1, D_MODEL), lambda i, j: (0, 0)),
            pl.BlockSpec((D_MODEL, INPROJ_BN), lambda i, j: (0, j)),
            pl.BlockSpec((1, INPROJ_BN), lambda i, j: (0, j)),
            pl.BlockSpec((WO_SLAB, D_MODEL), wo_index),
        ],
        out_specs=[
            pl.BlockSpec((INPROJ_BM, INPROJ_BN),
                         lambda i, j: (jnp.maximum(i - 1, 0), jnp.where(i == 0, 0, j))),
            pl.BlockSpec((WO_SLAB, D_MODEL), wo_index),
        ],
        out_shape=[
            jax.ShapeDtypeStruct((SEQ, IN_COLS), BF16),
            jax.ShapeDtypeStruct((D_MODEL, D_MODEL), BF16),
        ],
        scratch_shapes=[
            pltpu.VMEM((INPROJ_BM, D_MODEL), BF16), pltpu.VMEM((INPROJ_BM, 1), F32),
            pltpu.VMEM((INPROJ_BM, D_MODEL), BF16), pltpu.VMEM((INPROJ_BM, 1), F32),
        ],
        compiler_params=_params("arbitrary", "arbitrary"),
    )(x, g1, w_in16, gain, w_out)


N_QBLOCKS = SEQ // BLOCK
KEYS = 3 * BLOCK
_K_COLBLOCK = ATTN_WIDTH // KV_WIDTH
_V_COLBLOCK = _K_COLBLOCK + 1


def _attn_kernel(sink_ref, q_ref, kp_ref, kc_ref, kn_ref, vp_ref, vc_ref, vn_ref, o_ref, bias_ref):
    n = pl.program_id(0)

    @pl.when(n == 0)
    def _():
        qi = lax.broadcasted_iota(jnp.int32, (BLOCK, KEYS), 0)
        kj = lax.broadcasted_iota(jnp.int32, (BLOCK, KEYS), 1)
        dist = jnp.abs(kj - BLOCK - qi)
        in_band = dist <= WINDOW
        distf = dist.astype(F32)
        for v, valid in enumerate((in_band, in_band & (kj >= BLOCK), in_band & (kj < 2 * BLOCK))):
            for h in range(N_HEADS):
                bias_ref[v * N_HEADS + h] = jnp.where(
                    valid, (-ALIBI_SLOPES[h] * LOG2E) * distf, NEG_INF)

    variant = jnp.where(n == 0, 1, jnp.where(n == N_QBLOCKS - 1, 2, 0))
    row_head = lax.broadcasted_iota(jnp.int32, (GROUP * BLOCK, 1), 0) // BLOCK

    for kh in range(N_KV_HEADS):
        ksl = slice(kh * HEAD_DIM, (kh + 1) * HEAD_DIM)
        kw = jnp.concatenate([kp_ref[:, ksl], kc_ref[:, ksl], kn_ref[:, ksl]], axis=0)
        vw = jnp.concatenate([vp_ref[:, ksl], vc_ref[:, ksl], vn_ref[:, ksl]], axis=0)
        h0 = kh * GROUP
        qs = jnp.concatenate(
            [q_ref[:, (h0 + g) * HEAD_DIM:(h0 + g + 1) * HEAD_DIM] for g in range(GROUP)], axis=0)
        s = lax.dot_general(qs, kw, (((1,), (1,)), ((), ())), preferred_element_type=F32)
        bias = bias_ref[pl.ds(variant * N_HEADS + h0, GROUP)].reshape(GROUP * BLOCK, KEYS)
        logits = s + bias
        sink = jnp.full((GROUP * BLOCK, 1), sink_ref[h0 + GROUP - 1] * LOG2E, F32)
        for g in range(GROUP - 1):
            sink = jnp.where(row_head == g, sink_ref[h0 + g] * LOG2E, sink)
        m = jnp.maximum(jnp.max(logits, axis=-1, keepdims=True), sink)
        e = jnp.exp2(logits - m)
        den = jnp.sum(e, axis=-1, keepdims=True) + jnp.exp2(sink - m)
        o = jnp.dot(e.astype(BF16), vw, preferred_element_type=F32) * (1.0 / den)
        for g in range(GROUP):
            o_ref[:, (h0 + g) * HEAD_DIM:(h0 + g + 1) * HEAD_DIM] = (
                o[g * BLOCK:(g + 1) * BLOCK].astype(BF16))


def _attention(sink_logits, proj):
    last = N_QBLOCKS - 1
    prev_i = lambda n: jnp.maximum(n - 1, 0)
    next_i = lambda n: jnp.minimum(n + 1, last)
    kv_spec = lambda row_fn, col: pl.BlockSpec((BLOCK, KV_WIDTH), lambda n: (row_fn(n), col))
    same = lambda n: n
    return pl.pallas_call(
        _attn_kernel,
        name="attn",
        grid=(N_QBLOCKS,),
        in_specs=[
            pl.BlockSpec(memory_space=pltpu.SMEM),
            pl.BlockSpec((BLOCK, ATTN_WIDTH), lambda n: (n, 0)),
            kv_spec(prev_i, _K_COLBLOCK), kv_spec(same, _K_COLBLOCK), kv_spec(next_i, _K_COLBLOCK),
            kv_spec(prev_i, _V_COLBLOCK), kv_spec(same, _V_COLBLOCK), kv_spec(next_i, _V_COLBLOCK),
        ],
        out_specs=pl.BlockSpec((BLOCK, ATTN_WIDTH), lambda n: (n, 0)),
        out_shape=jax.ShapeDtypeStruct((SEQ, ATTN_WIDTH), BF16),
        scratch_shapes=[pltpu.VMEM((3 * N_HEADS, BLOCK, KEYS), F32)],
        compiler_params=_params("arbitrary"),
    )(sink_logits, proj, proj, proj, proj, proj, proj, proj)


POOL_BM = 512
_P_COLBLOCK = (ATTN_WIDTH + 2 * KV_WIDTH) // POOL_GROUP_W


def _pool_kernel(pp_ref, pc_ref, pn_ref, w_ref, sc_ref, o_ref):
    gi = pl.program_id(0)
    i = pl.program_id(1)
    ext = POOL_BM + 2 * POOL_HALO
    left = jnp.left_shift(1, gi)
    right = left - 1
    t = lax.broadcasted_iota(jnp.int32, (POOL_BM, ext), 0)
    s = lax.broadcasted_iota(jnp.int32, (POOL_BM, ext), 1) - POOL_HALO
    s_glob = i * POOL_BM + s
    member = (s >= t - left) & (s <= t + right) & (s_glob >= 0) & (s_glob < SEQ)
    band = jnp.where(member, 1.0, 0.0).astype(BF16)
    p_cur = pc_ref[...]
    p_ext = jnp.concatenate([pp_ref[...], p_cur, pn_ref[...]], axis=0)
    win_sum = jnp.dot(band, p_ext, preferred_element_type=F32)
    tg = i * POOL_BM + lax.broadcasted_iota(jnp.int32, (POOL_BM, 1), 0)
    cnt = jnp.minimum(tg + right + 1, SEQ) - jnp.maximum(tg - left, 0)
    u = win_sum * (1.0 / cnt.astype(F32)) - p_cur.astype(F32)
    y = jnp.dot(u.astype(BF16), w_ref[0], preferred_element_type=F32)
    o_ref[...] = (y * sc_ref[...]).astype(BF16)


def _pool(proj, pool_w, pool_scale):
    rows_per_halo = POOL_BM // POOL_HALO
    n_halo_blocks = SEQ // POOL_HALO
    return pl.pallas_call(
        _pool_kernel,
        name="pool",
        grid=(N_POOL_GROUPS, SEQ // POOL_BM),
        in_specs=[
            pl.BlockSpec((POOL_HALO, POOL_GROUP_W),
                         lambda g, i: (jnp.maximum(i * rows_per_halo - 1, 0), _P_COLBLOCK + g)),
            pl.BlockSpec((POOL_BM, POOL_GROUP_W), lambda g, i: (i, _P_COLBLOCK + g)),
            pl.BlockSpec((POOL_HALO, POOL_GROUP_W),
                         lambda g, i: (jnp.minimum((i + 1) * rows_per_halo, n_halo_blocks - 1),
                                       _P_COLBLOCK + g)),
            pl.BlockSpec((1, POOL_GROUP_W, POOL_GROUP_W), lambda g, i: (g, 0, 0)),
            pl.BlockSpec((1, POOL_GROUP_W), lambda g, i: (0, g)),
        ],
        out_specs=pl.BlockSpec((POOL_BM, POOL_GROUP_W), lambda g, i: (i, g)),
        out_shape=jax.ShapeDtypeStruct((SEQ, POOL_WIDTH), BF16),
        compiler_params=_params("arbitrary", "arbitrary"),
    )(proj, proj, proj, pool_w, pool_scale)


OUTPROJ_BM = 1024
OUTPROJ_BN = 512
_OUTPROJ_NJ = D_MODEL // OUTPROJ_BN
_OUT_MROWS = 512


def _outproj_kernel(at_ref, po_ref, w_ref, x_ref, g2_ref, x1_ref, xg_ref, rs_ref, ss_ref):
    j = pl.program_id(1)
    @pl.when(j == 0)
    def _():
        ss_ref[...] = jnp.zeros_like(ss_ref)

    for r in range(OUTPROJ_BM // _OUT_MROWS):
        rows = slice(r * _OUT_MROWS, (r + 1) * _OUT_MROWS)
        for c in range(OUTPROJ_BN // MXU_N):
            cs = slice(c * MXU_N, (c + 1) * MXU_N)
            acc = jnp.dot(at_ref[rows, :], w_ref[:ATTN_WIDTH, cs], preferred_element_type=F32)
            acc = acc + jnp.dot(po_ref[rows, :], w_ref[ATTN_WIDTH:, cs],
                                preferred_element_type=F32)
            x1 = x_ref[rows, cs] + acc
            x1_ref[rows, cs] = x1
            xg_ref[rows, cs] = (x1 * g2_ref[:, cs]).astype(BF16)
            sq = x1 * x1
            ss_ref[rows, :] += sum(sq[:, k * LANES:(k + 1) * LANES] for k in range(MXU_N // LANES))

    @pl.when(j == _OUTPROJ_NJ - 1)
    def _():
        ms = jnp.sum(ss_ref[...], axis=-1, keepdims=True) * (1.0 / D_MODEL)
        rs_ref[...] = lax.rsqrt(ms + RMS_EPS)


def _outproj(attn, pool, w_out16, x, g2):
    return pl.pallas_call(
        _outproj_kernel,
        name="outproj",
        grid=(SEQ // OUTPROJ_BM, _OUTPROJ_NJ),
        in_specs=[
            pl.BlockSpec((OUTPROJ_BM, ATTN_WIDTH), lambda i, j: (i, 0)),
            pl.BlockSpec((OUTPROJ_BM, POOL_WIDTH), lambda i, j: (i, 0)),
            pl.BlockSpec((D_MODEL, OUTPROJ_BN), lambda i, j: (0, j)),
            pl.BlockSpec((OUTPROJ_BM, OUTPROJ_BN), lambda i, j: (i, j)),
            pl.BlockSpec((1, OUTPROJ_BN), lambda i, j: (0, j)),
        ],
        out_specs=[
            pl.BlockSpec((OUTPROJ_BM, OUTPROJ_BN), lambda i, j: (i, j)),
            pl.BlockSpec((OUTPROJ_BM, OUTPROJ_BN), lambda i, j: (i, j)),
            pl.BlockSpec((OUTPROJ_BM, 1), lambda i, j: (i, 0)),
        ],
        out_shape=[
            jax.ShapeDtypeStruct((SEQ, D_MODEL), F32),
            jax.ShapeDtypeStruct((SEQ, D_MODEL), BF16),
            jax.ShapeDtypeStruct((SEQ, 1), F32),
        ],
        scratch_shapes=[pltpu.VMEM((OUTPROJ_BM, LANES), F32)],
        compiler_params=_params("arbitrary", "arbitrary"),
    )(attn, pool, w_out16, x, g2)


UP_BM = 2048
FFN_BF = 256
_N_FF_BLOCKS = D_FF // FFN_BF
DOWN_BM = 512
DOWN_BN = 512


def _ffn_up_kernel(xg_ref, rs_ref, wg_ref, wu_ref, wd_ref, a_ref, wd16_ref):
    xg = xg_ref[...]
    rs = rs_ref[...]
    g = jnp.dot(xg, wg_ref[...].astype(BF16), preferred_element_type=F32) * rs
    u = jnp.dot(xg, wu_ref[...].astype(BF16), preferred_element_type=F32) * rs
    a_ref[...] = (g * jax.nn.sigmoid(g) * u).astype(BF16)

    @pl.when(pl.program_id(0) == 0)
    def _():
        wd16_ref[...] = wd_ref[...].astype(BF16)


def _ffn_up(xg, rs2, w_gate, w_up, w_down):
    wd_index = lambda i, f: (jnp.where(i == 0, f, _N_FF_BLOCKS - 1), 0)
    return pl.pallas_call(
        _ffn_up_kernel,
        name="ffn_up",
        grid=(SEQ // UP_BM, _N_FF_BLOCKS),
        in_specs=[
            pl.BlockSpec((UP_BM, D_MODEL), lambda i, f: (i, 0), pipeline_mode=pl.Buffered(1)),
            pl.BlockSpec((UP_BM, 1), lambda i, f: (i, 0), pipeline_mode=pl.Buffered(1)),
            pl.BlockSpec((D_MODEL, FFN_BF), lambda i, f: (0, f)),
            pl.BlockSpec((D_MODEL, FFN_BF), lambda i, f: (0, f)),
            pl.BlockSpec((FFN_BF, D_MODEL), wd_index),
        ],
        out_specs=[
            pl.BlockSpec((UP_BM, FFN_BF), lambda i, f: (i, f)),
            pl.BlockSpec((FFN_BF, D_MODEL), wd_index),
        ],
        out_shape=[
            jax.ShapeDtypeStruct((SEQ, D_FF), BF16),
            jax.ShapeDtypeStruct((D_FF, D_MODEL), BF16),
        ],
        compiler_params=_params("arbitrary", "arbitrary"),
    )(xg, rs2, w_gate, w_up, w_down)


def _ffn_down_kernel(a_ref, w_ref, x1_ref, o_ref):
    for c in range(DOWN_BN // MXU_N):
        cs = slice(c * MXU_N, (c + 1) * MXU_N)
        o_ref[:, cs] = x1_ref[:, cs] + jnp.dot(a_ref[...], w_ref[:, cs], preferred_element_type=F32)


def _ffn_down(a, w_down16, x1):
    return pl.pallas_call(
        _ffn_down_kernel,
        name="ffn_down",
        grid=(SEQ // DOWN_BM, D_MODEL // DOWN_BN),
        in_specs=[
            pl.BlockSpec((DOWN_BM, D_FF), lambda i, j: (i, 0)),
            pl.BlockSpec((D_FF, DOWN_BN), lambda i, j: (0, j)),
            pl.BlockSpec((DOWN_BM, DOWN_BN), lambda i, j: (i, j)),
        ],
        out_specs=pl.BlockSpec((DOWN_BM, DOWN_BN), lambda i, j: (i, j)),
        out_shape=jax.ShapeDtypeStruct((SEQ, D_MODEL), F32),
        compiler_params=_params("arbitrary", "arbitrary"),
    )(a, w_down16, x1)


def kernel(x, norm1_g, w_in, q_norm_g, k_norm_g, sink_logits, pool_w, pool_scale, w_out, norm2_g,
           w_gate, w_up, w_down):
    b, s, d = x.shape
    assert (b, s, d) == (1, SEQ, D_MODEL)
    x2 = x.reshape(s, d)
    gain = jnp.concatenate([
        jnp.tile(q_norm_g, N_HEADS) * (Q_SCALE * LOG2E),
        jnp.tile(k_norm_g, N_KV_HEADS),
        jnp.ones((KV_WIDTH + POOL_WIDTH,), F32),
    ]).reshape(1, IN_COLS)
    proj, w_out16 = _inproj(x2, norm1_g.reshape(1, D_MODEL), w_in.astype(BF16), gain, w_out)
    attn = _attention(sink_logits, proj)
    pool = _pool(proj, pool_w.astype(BF16), pool_scale.reshape(1, POOL_WIDTH))
    x1, xg, rs2 = _outproj(attn, pool, w_out16, x2, norm2_g.reshape(1, D_MODEL))
    a, w_down16 = _ffn_up(xg, rs2, w_gate, w_up, w_down)
    out = _ffn_down(a, w_down16, x1)
    return out.reshape(b, s, d)
```

```python
import jax
import jax.numpy as jnp
from jax import lax
from jax.experimental import pallas as pl
from jax.experimental.pallas import tpu as pltpu

F32 = jnp.float32
BF16 = jnp.bfloat16

D_MODEL = 4096
SEQ = 8192
HEAD_DIM = 128
N_HEADS = 16
N_KV_HEADS = 4
GROUP = N_HEADS // N_KV_HEADS
ATTN_WIDTH = N_HEADS * HEAD_DIM
KV_WIDTH = N_KV_HEADS * HEAD_DIM
POOL_WIDTH = D_MODEL - ATTN_WIDTH
POOL_WINDOWS = (2, 4, 8, 16)
N_POOL_GROUPS = len(POOL_WINDOWS)
POOL_GROUP_W = POOL_WIDTH // N_POOL_GROUPS
IN_COLS = ATTN_WIDTH + 2 * KV_WIDTH + POOL_WIDTH
D_FF = 11008
WINDOW = 128
BLOCK = 128
RMS_EPS = 1e-6
NEG_INF = -1e30
Q_SCALE = HEAD_DIM ** -0.5
LOG2E = 1.4426950408889634
ALIBI_SLOPES = tuple(2.0 ** (-8.0 * h / N_HEADS) for h in range(1, N_HEADS + 1))

VMEM_LIMIT_BYTES = 56 * 1024 * 1024
MXU_N = 256
LANES = 128
POOL_HALO = 16


def _params(*semantics):
    return pltpu.CompilerParams(dimension_semantics=semantics, vmem_limit_bytes=VMEM_LIMIT_BYTES)


INPROJ_BM = 1024
INPROJ_BN = 1024
_IN_NI = SEQ // INPROJ_BM
_IN_NJ = IN_COLS // INPROJ_BN
_IN_SUBS = 4
_IN_SUB_ROWS = INPROJ_BM // _IN_SUBS
_IN_MROWS = 512
_Q_GROUPS = ATTN_WIDTH // HEAD_DIM
_K_GROUP0 = (ATTN_WIDTH + POOL_WIDTH) // HEAD_DIM
_K_GROUPS = N_KV_HEADS
WO_SLAB = 128
_WO_SLABS = D_MODEL // WO_SLAB


def _inproj_kernel(x_ref, g1_ref, w_ref, gain_ref, wo_ref, o_ref, wo16_ref,
                   h_a, rs_a, h_b, rs_b):
    i = pl.program_id(0)
    j = pl.program_id(1)
    row0 = pl.multiple_of(jnp.minimum(j, _IN_SUBS - 1) * _IN_SUB_ROWS, _IN_SUB_ROWS)

    def stage_rows(h_ref, rs_ref):
        x = x_ref[...]
        ms = jnp.mean(x * x, axis=-1, keepdims=True)
        rs_ref[pl.ds(row0, _IN_SUB_ROWS), :] = lax.rsqrt(ms + RMS_EPS)
        h_ref[pl.ds(row0, _IN_SUB_ROWS), :] = (x * g1_ref[...]).astype(BF16)

    def project(h_ref, rs_ref):
        for c in range(INPROJ_BN // MXU_N):
            for r in range(INPROJ_BM // _IN_MROWS):
                rows = slice(r * _IN_MROWS, (r + 1) * _IN_MROWS)
                acc = jnp.dot(h_ref[rows, :], w_ref[:, c * MXU_N:(c + 1) * MXU_N],
                              preferred_element_type=F32) * rs_ref[rows, :]
                for hh in range(MXU_N // HEAD_DIM):
                    col = c * MXU_N + hh * HEAD_DIM
                    head = j * (INPROJ_BN // HEAD_DIM) + col // HEAD_DIM
                    blk = acc[:, hh * HEAD_DIM:(hh + 1) * HEAD_DIM]
                    ms = jnp.mean(blk * blk, axis=-1, keepdims=True)
                    normed = (head < _Q_GROUPS) | ((head >= _K_GROUP0) & (head < _K_GROUP0 + _K_GROUPS))
                    scale = jnp.where(normed, lax.rsqrt(ms + RMS_EPS), 1.0)
                    o_ref[rows, col:col + HEAD_DIM] = (
                        blk * scale * gain_ref[:, col:col + HEAD_DIM]).astype(BF16)

    @pl.when(i == 0)
    def _():
        stage_rows(h_a, rs_a)

    @pl.when((i > 0) & (i % 2 == 1))
    def _():
        stage_rows(h_b, rs_b)
        project(h_a, rs_a)

    @pl.when((i > 0) & (i % 2 == 0))
    def _():
        stage_rows(h_a, rs_a)
        project(h_b, rs_b)

    @pl.when(i * _IN_NJ + j < _WO_SLABS)
    def _():
        wo16_ref[...] = wo_ref[...].astype(BF16)


def _inproj(x, g1, w_in16, gain, w_out):
    assert _IN_SUBS <= _IN_NJ and _WO_SLABS <= (_IN_NI + 1) * _IN_NJ
    x_index = lambda i, j: (jnp.minimum(i, _IN_NI - 1) * _IN_SUBS + jnp.minimum(j, _IN_SUBS - 1), 0)
    wo_index = lambda i, j: (jnp.minimum(i * _IN_NJ + j, _WO_SLABS - 1), 0)
    return pl.pallas_call(
        _inproj_kernel,
        name="inproj",
        grid=(_IN_NI + 1, _IN_NJ),
        in_specs=[
            pl.BlockSpec((_IN_SUB_ROWS, D_MODEL), x_index),
            pl.BlockSpec((1, D_MODEL), lambda i, j: (0, 0)),
            pl.BlockSpec((D_MODEL, INPROJ_BN), lambda i, j: (0, j)),
            pl.BlockSpec((1, INPROJ_BN), lambda i, j: (0, j)),
            pl.BlockSpec((WO_SLAB, D_MODEL), wo_index),
        ],
        out_specs=[
            pl.BlockSpec((INPROJ_BM, INPROJ_BN),
                         lambda i, j: (jnp.maximum(i - 1, 0), jnp.where(i == 0, 0, j))),
            pl.BlockSpec((WO_SLAB, D_MODEL), wo_index),
        ],
        out_shape=[
            jax.ShapeDtypeStruct((SEQ, IN_COLS), BF16),
            jax.ShapeDtypeStruct((D_MODEL, D_MODEL), BF16),
        ],
        scratch_shapes=[
            pltpu.VMEM((INPROJ_BM, D_MODEL), BF16), pltpu.VMEM((INPROJ_BM, 1), F32),
            pltpu.VMEM((INPROJ_BM, D_MODEL), BF16), pltpu.VMEM((INPROJ_BM, 1), F32),
        ],
        compiler_params=_params("arbitrary", "arbitrary"),
    )(x, g1, w_in16, gain, w_out)


MIX_ROWS = 2 * BLOCK
_MIX_STEPS = SEQ // MIX_ROWS
_MIX_EXT = MIX_ROWS + 2 * POOL_HALO
KEYS = 3 * BLOCK
_P_COLBLOCK = ATTN_WIDTH // POOL_WIDTH
_K_COLBLOCK = (ATTN_WIDTH + POOL_WIDTH) // KV_WIDTH
_V_COLBLOCK = _K_COLBLOCK + 1


def _mixers_kernel(sink_ref, q_ref, kp_ref, kc_ref, kn_ref, vp_ref, vc_ref, vn_ref,
                   pp_ref, pc_ref, pn_ref, pw_ref, sc_ref, o_ref, bias_ref, band_ref):
    n = pl.program_id(0)

    @pl.when(n == 0)
    def _():
        qi = lax.broadcasted_iota(jnp.int32, (BLOCK, KEYS), 0)
        kj = lax.broadcasted_iota(jnp.int32, (BLOCK, KEYS), 1)
        dist = jnp.abs(kj - BLOCK - qi)
        in_band = dist <= WINDOW
        distf = dist.astype(F32)
        for v, valid in enumerate((in_band, in_band & (kj >= BLOCK), in_band & (kj < 2 * BLOCK))):
            for h in range(N_HEADS):
                bias_ref[v * N_HEADS + h] = jnp.where(
                    valid, (-ALIBI_SLOPES[h] * LOG2E) * distf, NEG_INF)

        t = lax.broadcasted_iota(jnp.int32, (MIX_ROWS, _MIX_EXT), 0)
        s_rel = lax.broadcasted_iota(jnp.int32, (MIX_ROWS, _MIX_EXT), 1) - POOL_HALO
        for gi, w in enumerate(POOL_WINDOWS):
            left = w // 2
            right = w - 1 - left
            band_ref[gi] = jnp.where((s_rel >= t - left) & (s_rel <= t + right), 1.0, 0.0).astype(BF16)

    is_first = n == 0
    is_last = n == _MIX_STEPS - 1
    row_head = lax.broadcasted_iota(jnp.int32, (GROUP * BLOCK, 1), 0) // BLOCK

    for qb in range(MIX_ROWS // BLOCK):
        qrows = slice(qb * BLOCK, (qb + 1) * BLOCK)
        if qb == 0:
            variant = jnp.where(is_first, 1, 0)
        else:
            variant = jnp.where(is_last, 2, 0)
        for kh in range(N_KV_HEADS):
            ksl = slice(kh * HEAD_DIM, (kh + 1) * HEAD_DIM)
            k_all = jnp.concatenate([kp_ref[:, ksl], kc_ref[:, ksl], kn_ref[:, ksl]], axis=0)
            v_all = jnp.concatenate([vp_ref[:, ksl], vc_ref[:, ksl], vn_ref[:, ksl]], axis=0)
            kw = k_all[qb * BLOCK:qb * BLOCK + KEYS]
            vw = v_all[qb * BLOCK:qb * BLOCK + KEYS]
            h0 = kh * GROUP
            qs = jnp.concatenate(
                [q_ref[qrows, (h0 + g) * HEAD_DIM:(h0 + g + 1) * HEAD_DIM] for g in range(GROUP)],
                axis=0)
            s = lax.dot_general(qs, kw, (((1,), (1,)), ((), ())), preferred_element_type=F32)
            bias = bias_ref[pl.ds(variant * N_HEADS + h0, GROUP)].reshape(GROUP * BLOCK, KEYS)
            logits = s + bias
            sink = jnp.full((GROUP * BLOCK, 1), sink_ref[h0 + GROUP - 1] * LOG2E, F32)
            for g in range(GROUP - 1):
                sink = jnp.where(row_head == g, sink_ref[h0 + g] * LOG2E, sink)
            m = jnp.maximum(jnp.max(logits, axis=-1, keepdims=True), sink)
            e = jnp.exp2(logits - m)
            den = jnp.sum(e, axis=-1, keepdims=True) + jnp.exp2(sink - m)
            o = jnp.dot(e.astype(BF16), vw, preferred_element_type=F32) * (1.0 / den)
            for g in range(GROUP):
                o_ref[qrows, (h0 + g) * HEAD_DIM:(h0 + g + 1) * HEAD_DIM] = (
                    o[g * BLOCK:(g + 1) * BLOCK].astype(BF16))

    p_cur = pc_ref[...]
    p_prev = jnp.where(is_first, jnp.zeros_like(pp_ref), pp_ref[...])
    p_next = jnp.where(is_last, jnp.zeros_like(pn_ref), pn_ref[...])
    p_ext = jnp.concatenate([p_prev, p_cur, p_next], axis=0)
    tg = n * MIX_ROWS + lax.broadcasted_iota(jnp.int32, (MIX_ROWS, 1), 0)
    for gi, w in enumerate(POOL_WINDOWS):
        left = w // 2
        right = w - 1 - left
        cs = slice(gi * POOL_GROUP_W, (gi + 1) * POOL_GROUP_W)
        win_sum = jnp.dot(band_ref[gi], p_ext[:, cs], preferred_element_type=F32)
        cnt = jnp.minimum(tg + right + 1, SEQ) - jnp.maximum(tg - left, 0)
        u = win_sum * (1.0 / cnt.astype(F32)) - p_cur[:, cs].astype(F32)
        y = jnp.dot(u.astype(BF16), pw_ref[gi], preferred_element_type=F32)
        o_ref[:, ATTN_WIDTH + gi * POOL_GROUP_W:ATTN_WIDTH + (gi + 1) * POOL_GROUP_W] = (
            y * sc_ref[:, cs]).astype(BF16)


def _mixers(sink_logits, proj, pool_w16, pool_scale):
    last_qblock = SEQ // BLOCK - 1
    last_halo = SEQ // POOL_HALO - 1
    qb_per_step = MIX_ROWS // BLOCK
    halo_per_step = MIX_ROWS // POOL_HALO
    kv_prev = lambda col: pl.BlockSpec(
        (BLOCK, KV_WIDTH), lambda n: (jnp.maximum(n * qb_per_step - 1, 0), col))
    kv_cur = lambda col: pl.BlockSpec((MIX_ROWS, KV_WIDTH), lambda n: (n, col))
    kv_next = lambda col: pl.BlockSpec(
        (BLOCK, KV_WIDTH), lambda n: (jnp.minimum((n + 1) * qb_per_step, last_qblock), col))
    return pl.pallas_call(
        _mixers_kernel,
        name="mixers",
        grid=(_MIX_STEPS,),
        in_specs=[
            pl.BlockSpec(memory_space=pltpu.SMEM),
            pl.BlockSpec((MIX_ROWS, ATTN_WIDTH), lambda n: (n, 0)),
            kv_prev(_K_COLBLOCK), kv_cur(_K_COLBLOCK), kv_next(_K_COLBLOCK),
            kv_prev(_V_COLBLOCK), kv_cur(_V_COLBLOCK), kv_next(_V_COLBLOCK),
            pl.BlockSpec((POOL_HALO, POOL_WIDTH),
                         lambda n: (jnp.maximum(n * halo_per_step - 1, 0), _P_COLBLOCK)),
            pl.BlockSpec((MIX_ROWS, POOL_WIDTH), lambda n: (n, _P_COLBLOCK)),
            pl.BlockSpec((POOL_HALO, POOL_WIDTH),
                         lambda n: (jnp.minimum((n + 1) * halo_per_step, last_halo), _P_COLBLOCK)),
            pl.BlockSpec((N_POOL_GROUPS, POOL_GROUP_W, POOL_GROUP_W), lambda n: (0, 0, 0)),
            pl.BlockSpec((1, POOL_WIDTH), lambda n: (0, 0)),
        ],
        out_specs=pl.BlockSpec((MIX_ROWS, D_MODEL), lambda n: (n, 0)),
        out_shape=jax.ShapeDtypeStruct((SEQ, D_MODEL), BF16),
        scratch_shapes=[
            pltpu.VMEM((3 * N_HEADS, BLOCK, KEYS), F32),
            pltpu.VMEM((N_POOL_GROUPS, MIX_ROWS, _MIX_EXT), BF16),
        ],
        compiler_params=_params("arbitrary"),
    )(sink_logits, proj, proj, proj, proj, proj, proj, proj, proj, proj, proj, pool_w16, pool_scale)


OUTPROJ_BM = 1024
OUTPROJ_BN = 1024
_OUTPROJ_NJ = D_MODEL // OUTPROJ_BN
_OUT_MROWS = 512


def _outproj_kernel(mx_ref, w_ref, x_ref, g2_ref, x1_ref, xg_ref, rs_ref, ss_ref):
    j = pl.program_id(1)
    @pl.when(j == 0)
    def _():
        ss_ref[...] = jnp.zeros_like(ss_ref)

    for r in range(OUTPROJ_BM // _OUT_MROWS):
        rows = slice(r * _OUT_MROWS, (r + 1) * _OUT_MROWS)
        for c in range(OUTPROJ_BN // MXU_N):
            cs = slice(c * MXU_N, (c + 1) * MXU_N)
            x1 = x_ref[rows, cs] + jnp.dot(mx_ref[rows, :], w_ref[:, cs],
                                           preferred_element_type=F32)
            x1_ref[rows, cs] = x1
            xg_ref[rows, cs] = (x1 * g2_ref[:, cs]).astype(BF16)
            sq = x1 * x1
            ss_ref[rows, :] += sum(sq[:, k * LANES:(k + 1) * LANES] for k in range(MXU_N // LANES))

    @pl.when(j == _OUTPROJ_NJ - 1)
    def _():
        ms = jnp.sum(ss_ref[...], axis=-1, keepdims=True) * (1.0 / D_MODEL)
        rs_ref[...] = lax.rsqrt(ms + RMS_EPS)


def _outproj(mixed, w_out16, x, g2):
    return pl.pallas_call(
        _outproj_kernel,
        name="outproj",
        grid=(SEQ // OUTPROJ_BM, _OUTPROJ_NJ),
        in_specs=[
            pl.BlockSpec((OUTPROJ_BM, D_MODEL), lambda i, j: (i, 0)),
            pl.BlockSpec((D_MODEL, OUTPROJ_BN), lambda i, j: (0, j)),
            pl.BlockSpec((OUTPROJ_BM, OUTPROJ_BN), lambda i, j: (i, j)),
            pl.BlockSpec((1, OUTPROJ_BN), lambda i, j: (0, j)),
        ],
        out_specs=[
            pl.BlockSpec((OUTPROJ_BM, OUTPROJ_BN), lambda i, j: (i, j)),
            pl.BlockSpec((OUTPROJ_BM, OUTPROJ_BN), lambda i, j: (i, j)),
            pl.BlockSpec((OUTPROJ_BM, 1), lambda i, j: (i, 0)),
        ],
        out_shape=[
            jax.ShapeDtypeStruct((SEQ, D_MODEL), F32),
            jax.ShapeDtypeStruct((SEQ, D_MODEL), BF16),
            jax.ShapeDtypeStruct((SEQ, 1), F32),
        ],
        scratch_shapes=[pltpu.VMEM((OUTPROJ_BM, LANES), F32)],
        compiler_params=_params("arbitrary", "arbitrary"),
    )(mixed, w_out16, x, g2)


UP_BM = 2048
_UP_MROWS = 1024
FFN_BF = 256
_N_FF_BLOCKS = D_FF // FFN_BF
DOWN_BM = 512
DOWN_BN = 512


def _ffn_up_kernel(xg_ref, rs_ref, wg_ref, wu_ref, wd_ref, a_ref, wd16_ref):
    wg = wg_ref[...].astype(BF16)
    wu = wu_ref[...].astype(BF16)
    for r in range(UP_BM // _UP_MROWS):
        rows = slice(r * _UP_MROWS, (r + 1) * _UP_MROWS)
        xg = xg_ref[rows, :]
        rs = rs_ref[rows, :]
        g = jnp.dot(xg, wg, preferred_element_type=F32) * rs
        u = jnp.dot(xg, wu, preferred_element_type=F32) * rs
        a_ref[rows, :] = (g * jax.nn.sigmoid(g) * u).astype(BF16)

    @pl.when(pl.program_id(0) == 0)
    def _():
        wd16_ref[...] = wd_ref[...].astype(BF16)


def _ffn_up(xg, rs2, w_gate, w_up, w_down):
    wd_index = lambda i, f: (jnp.where(i == 0, f, _N_FF_BLOCKS - 1), 0)
    return pl.pallas_call(
        _ffn_up_kernel,
        name="ffn_up",
        grid=(SEQ // UP_BM, _N_FF_BLOCKS),
        in_specs=[
            pl.BlockSpec((UP_BM, D_MODEL), lambda i, f: (i, 0), pipeline_mode=pl.Buffered(1)),
            pl.BlockSpec((UP_BM, 1), lambda i, f: (i, 0), pipeline_mode=pl.Buffered(1)),
            pl.BlockSpec((D_MODEL, FFN_BF), lambda i, f: (0, f)),
            pl.BlockSpec((D_MODEL, FFN_BF), lambda i, f: (0, f)),
            pl.BlockSpec((FFN_BF, D_MODEL), wd_index),
        ],
        out_specs=[
            pl.BlockSpec((UP_BM, FFN_BF), lambda i, f: (i, f)),
            pl.BlockSpec((FFN_BF, D_MODEL), wd_index),
        ],
        out_shape=[
            jax.ShapeDtypeStruct((SEQ, D_FF), BF16),
            jax.ShapeDtypeStruct((D_FF, D_MODEL), BF16),
        ],
        compiler_params=_params("arbitrary", "arbitrary"),
    )(xg, rs2, w_gate, w_up, w_down)


def _ffn_down_kernel(a_ref, w_ref, x1_ref, o_ref):
    for c in range(DOWN_BN // MXU_N):
        cs = slice(c * MXU_N, (c + 1) * MXU_N)
        o_ref[:, cs] = x1_ref[:, cs] + jnp.dot(a_ref[...], w_ref[:, cs], preferred_element_type=F32)


def _ffn_down(a, w_down16, x1):
    return pl.pallas_call(
        _ffn_down_kernel,
        name="ffn_down",
        grid=(SEQ // DOWN_BM, D_MODEL // DOWN_BN),
        in_specs=[
            pl.BlockSpec((DOWN_BM, D_FF), lambda i, j: (i, 0)),
            pl.BlockSpec((D_FF, DOWN_BN), lambda i, j: (0, j)),
            pl.BlockSpec((DOWN_BM, DOWN_BN), lambda i, j: (i, j)),
        ],
        out_specs=pl.BlockSpec((DOWN_BM, DOWN_BN), lambda i, j: (i, j)),
        out_shape=jax.ShapeDtypeStruct((SEQ, D_MODEL), F32),
        compiler_params=_params("arbitrary", "arbitrary"),
    )(a, w_down16, x1)


def kernel(x, norm1_g, w_in, q_norm_g, k_norm_g, sink_logits, pool_w, pool_scale, w_out, norm2_g,
           w_gate, w_up, w_down):
    b, s, d = x.shape
    assert (b, s, d) == (1, SEQ, D_MODEL)
    x2 = x.reshape(s, d)
    gain = jnp.concatenate([
        jnp.tile(q_norm_g, N_HEADS) * (Q_SCALE * LOG2E),
        jnp.ones((POOL_WIDTH,), F32),
        jnp.tile(k_norm_g, N_KV_HEADS),
        jnp.ones((KV_WIDTH,), F32),
    ]).reshape(1, IN_COLS)
    kv0 = ATTN_WIDTH
    w_in16 = jnp.concatenate(
        [w_in[:, :kv0], w_in[:, kv0 + 2 * KV_WIDTH:], w_in[:, kv0:kv0 + 2 * KV_WIDTH]],
        axis=1).astype(BF16)
    proj, w_out16 = _inproj(x2, norm1_g.reshape(1, D_MODEL), w_in16, gain, w_out)
    mixed = _mixers(sink_logits, proj, pool_w.astype(BF16), pool_scale.reshape(1, POOL_WIDTH))
    x1, xg, rs2 = _outproj(mixed, w_out16, x2, norm2_g.reshape(1, D_MODEL))
    a, w_down16 = _ffn_up(xg, rs2, w_gate, w_up, w_down)
    out = _ffn_down(a, w_down16, x1)
    return out.reshape(b, s, d)
```

```python
import jax
import jax.numpy as jnp
from jax import lax
from jax.experimental import pallas as pl
from jax.experimental.pallas import tpu as pltpu

F32 = jnp.float32
BF16 = jnp.bfloat16

D_MODEL = 4096
SEQ = 8192
HEAD_DIM = 128
N_HEADS = 16
N_KV_HEADS = 4
GROUP = N_HEADS // N_KV_HEADS
ATTN_WIDTH = N_HEADS * HEAD_DIM
KV_WIDTH = N_KV_HEADS * HEAD_DIM
POOL_WIDTH = D_MODEL - ATTN_WIDTH
POOL_WINDOWS = (2, 4, 8, 16)
N_POOL_GROUPS = len(POOL_WINDOWS)
POOL_GROUP_W = POOL_WIDTH // N_POOL_GROUPS
IN_COLS = ATTN_WIDTH + 2 * KV_WIDTH + POOL_WIDTH
D_FF = 11008
WINDOW = 128
BLOCK = 128
RMS_EPS = 1e-6
NEG_INF = -1e30
Q_SCALE = HEAD_DIM ** -0.5
LOG2E = 1.4426950408889634
ALIBI_SLOPES = tuple(2.0 ** (-8.0 * h / N_HEADS) for h in range(1, N_HEADS + 1))

VMEM_LIMIT_BYTES = 56 * 1024 * 1024
MXU_N = 256
LANES = 128
POOL_HALO = 16


def _params(*semantics):
    return pltpu.CompilerParams(dimension_semantics=semantics, vmem_limit_bytes=VMEM_LIMIT_BYTES)


INPROJ_BM = 1024
INPROJ_BN = 1024
_IN_NI = SEQ // INPROJ_BM
_IN_NJ = IN_COLS // INPROJ_BN
_IN_SUBS = 4
_IN_SUB_ROWS = INPROJ_BM // _IN_SUBS
_IN_MROWS = 512
_Q_GROUPS = ATTN_WIDTH // HEAD_DIM
_K_GROUP0 = (ATTN_WIDTH + POOL_WIDTH) // HEAD_DIM
_K_GROUPS = N_KV_HEADS
WO_SLAB = 128
_WO_SLABS = D_MODEL // WO_SLAB


def _inproj_kernel(x_ref, g1_ref, w_ref, gain_ref, wo_ref, o_ref, wo16_ref,
                   h_a, rs_a, h_b, rs_b):
    i = pl.program_id(0)
    j = pl.program_id(1)
    row0 = pl.multiple_of(jnp.minimum(j, _IN_SUBS - 1) * _IN_SUB_ROWS, _IN_SUB_ROWS)

    def stage_rows(h_ref, rs_ref):
        x = x_ref[...]
        ms = jnp.mean(x * x, axis=-1, keepdims=True)
        rs_ref[pl.ds(row0, _IN_SUB_ROWS), :] = lax.rsqrt(ms + RMS_EPS)
        h_ref[pl.ds(row0, _IN_SUB_ROWS), :] = (x * g1_ref[...]).astype(BF16)

    def project(h_ref, rs_ref):
        for c in range(INPROJ_BN // MXU_N):
            for r in range(INPROJ_BM // _IN_MROWS):
                rows = slice(r * _IN_MROWS, (r + 1) * _IN_MROWS)
                acc = jnp.dot(h_ref[rows, :], w_ref[:, c * MXU_N:(c + 1) * MXU_N],
                              preferred_element_type=F32) * rs_ref[rows, :]
                for hh in range(MXU_N // HEAD_DIM):
                    col = c * MXU_N + hh * HEAD_DIM
                    head = j * (INPROJ_BN // HEAD_DIM) + col // HEAD_DIM
                    blk = acc[:, hh * HEAD_DIM:(hh + 1) * HEAD_DIM]
                    ms = jnp.mean(blk * blk, axis=-1, keepdims=True)
                    normed = (head < _Q_GROUPS) | ((head >= _K_GROUP0) & (head < _K_GROUP0 + _K_GROUPS))
                    scale = jnp.where(normed, lax.rsqrt(ms + RMS_EPS), 1.0)
                    o_ref[rows, col:col + HEAD_DIM] = (
                        blk * scale * gain_ref[:, col:col + HEAD_DIM]).astype(BF16)

    @pl.when(i == 0)
    def _():
        stage_rows(h_a, rs_a)

    @pl.when((i > 0) & (i % 2 == 1))
    def _():
        stage_rows(h_b, rs_b)
        project(h_a, rs_a)

    @pl.when((i > 0) & (i % 2 == 0))
    def _():
        stage_rows(h_a, rs_a)
        project(h_b, rs_b)

    @pl.when(i * _IN_NJ + j < _WO_SLABS)
    def _():
        wo16_ref[...] = wo_ref[...].astype(BF16)


def _inproj(x, g1, w_in16, gain, w_out):
    assert _IN_SUBS <= _IN_NJ and _WO_SLABS <= (_IN_NI + 1) * _IN_NJ
    x_index = lambda i, j: (jnp.minimum(i, _IN_NI - 1) * _IN_SUBS + jnp.minimum(j, _IN_SUBS - 1), 0)
    wo_index = lambda i, j: (jnp.minimum(i * _IN_NJ + j, _WO_SLABS - 1), 0)
    q_blocks = ATTN_WIDTH // INPROJ_BN
    p_blocks = POOL_WIDTH // INPROJ_BN
    assert q_blocks * INPROJ_BN == ATTN_WIDTH and 2 * KV_WIDTH == INPROJ_BN
    w_col = lambda j: jnp.where(j < q_blocks, j,
                                jnp.where(j < q_blocks + p_blocks, j + 1, q_blocks))
    return pl.pallas_call(
        _inproj_kernel,
        name="inproj",
        grid=(_IN_NI + 1, _IN_NJ),
        in_specs=[
            pl.BlockSpec((_IN_SUB_ROWS, D_MODEL), x_index),
            pl.BlockSpec((1, D_MODEL), lambda i, j: (0, 0)),
            pl.BlockSpec((D_MODEL, INPROJ_BN), lambda i, j: (0, w_col(j))),
            pl.BlockSpec((1, INPROJ_BN), lambda i, j: (0, j)),
            pl.BlockSpec((WO_SLAB, D_MODEL), wo_index),
        ],
        out_specs=[
            pl.BlockSpec((INPROJ_BM, INPROJ_BN),
                         lambda i, j: (jnp.maximum(i - 1, 0), jnp.where(i == 0, 0, j))),
            pl.BlockSpec((WO_SLAB, D_MODEL), wo_index),
        ],
        out_shape=[
            jax.ShapeDtypeStruct((SEQ, IN_COLS), BF16),
            jax.ShapeDtypeStruct((D_MODEL, D_MODEL), BF16),
        ],
        scratch_shapes=[
            pltpu.VMEM((INPROJ_BM, D_MODEL), BF16), pltpu.VMEM((INPROJ_BM, 1), F32),
            pltpu.VMEM((INPROJ_BM, D_MODEL), BF16), pltpu.VMEM((INPROJ_BM, 1), F32),
        ],
        compiler_params=_params("arbitrary", "arbitrary"),
    )(x, g1, w_in16, gain, w_out)


MIX_ROWS = 2 * BLOCK
_MIX_STEPS = SEQ // MIX_ROWS
_MIX_EXT = MIX_ROWS + 2 * POOL_HALO
KEYS = 3 * BLOCK
_P_COLBLOCK = ATTN_WIDTH // POOL_WIDTH
_K_COLBLOCK = (ATTN_WIDTH + POOL_WIDTH) // KV_WIDTH
_V_COLBLOCK = _K_COLBLOCK + 1


def _mixers_kernel(sink_ref, q_ref, kp_ref, kc_ref, kn_ref, vp_ref, vc_ref, vn_ref,
                   pp_ref, pc_ref, pn_ref, pw_ref, sc_ref, o_ref, bias_ref, band_ref):
    n = pl.program_id(0)

    @pl.when(n == 0)
    def _():
        qi = lax.broadcasted_iota(jnp.int32, (BLOCK, KEYS), 0)
        kj = lax.broadcasted_iota(jnp.int32, (BLOCK, KEYS), 1)
        dist = jnp.abs(kj - BLOCK - qi)
        in_band = dist <= WINDOW
        distf = dist.astype(F32)
        for v, valid in enumerate((in_band, in_band & (kj >= BLOCK), in_band & (kj < 2 * BLOCK))):
            for h in range(N_HEADS):
                bias_ref[v * N_HEADS + h] = jnp.where(
                    valid, (-ALIBI_SLOPES[h] * LOG2E) * distf, NEG_INF)

        t = lax.broadcasted_iota(jnp.int32, (MIX_ROWS, _MIX_EXT), 0)
        s_rel = lax.broadcasted_iota(jnp.int32, (MIX_ROWS, _MIX_EXT), 1) - POOL_HALO
        for gi, w in enumerate(POOL_WINDOWS):
            left = w // 2
            right = w - 1 - left
            band_ref[gi] = jnp.where((s_rel >= t - left) & (s_rel <= t + right), 1.0, 0.0).astype(BF16)

    is_first = n == 0
    is_last = n == _MIX_STEPS - 1
    row_head = lax.broadcasted_iota(jnp.int32, (GROUP * BLOCK, 1), 0) // BLOCK

    for qb in range(MIX_ROWS // BLOCK):
        qrows = slice(qb * BLOCK, (qb + 1) * BLOCK)
        if qb == 0:
            variant = jnp.where(is_first, 1, 0)
        else:
            variant = jnp.where(is_last, 2, 0)
        for kh in range(N_KV_HEADS):
            ksl = slice(kh * HEAD_DIM, (kh + 1) * HEAD_DIM)
            k_all = jnp.concatenate([kp_ref[:, ksl], kc_ref[:, ksl], kn_ref[:, ksl]], axis=0)
            v_all = jnp.concatenate([vp_ref[:, ksl], vc_ref[:, ksl], vn_ref[:, ksl]], axis=0)
            kw = k_all[qb * BLOCK:qb * BLOCK + KEYS]
            vw = v_all[qb * BLOCK:qb * BLOCK + KEYS]
            h0 = kh * GROUP
            qs = jnp.concatenate(
                [q_ref[qrows, (h0 + g) * HEAD_DIM:(h0 + g + 1) * HEAD_DIM] for g in range(GROUP)],
                axis=0)
            s = lax.dot_general(qs, kw, (((1,), (1,)), ((), ())), preferred_element_type=F32)
            bias = bias_ref[pl.ds(variant * N_HEADS + h0, GROUP)].reshape(GROUP * BLOCK, KEYS)
            logits = s + bias
            sink = jnp.full((GROUP * BLOCK, 1), sink_ref[h0 + GROUP - 1] * LOG2E, F32)
            for g in range(GROUP - 1):
                sink = jnp.where(row_head == g, sink_ref[h0 + g] * LOG2E, sink)
            m = jnp.maximum(jnp.max(logits, axis=-1, keepdims=True), sink)
            e = jnp.exp2(logits - m)
            den = jnp.sum(e, axis=-1, keepdims=True) + jnp.exp2(sink - m)
            o = jnp.dot(e.astype(BF16), vw, preferred_element_type=F32) * (1.0 / den)
            for g in range(GROUP):
                o_ref[qrows, (h0 + g) * HEAD_DIM:(h0 + g + 1) * HEAD_DIM] = (
                    o[g * BLOCK:(g + 1) * BLOCK].astype(BF16))

    p_cur = pc_ref[...]
    p_prev = jnp.where(is_first, jnp.zeros_like(pp_ref), pp_ref[...])
    p_next = jnp.where(is_last, jnp.zeros_like(pn_ref), pn_ref[...])
    p_ext = jnp.concatenate([p_prev, p_cur, p_next], axis=0)
    tg = n * MIX_ROWS + lax.broadcasted_iota(jnp.int32, (MIX_ROWS, 1), 0)
    for gi, w in enumerate(POOL_WINDOWS):
        left = w // 2
        right = w - 1 - left
        cs = slice(gi * POOL_GROUP_W, (gi + 1) * POOL_GROUP_W)
        win_sum = jnp.dot(band_ref[gi], p_ext[:, cs], preferred_element_type=F32)
        cnt = jnp.minimum(tg + right + 1, SEQ) - jnp.maximum(tg - left, 0)
        u = win_sum * (1.0 / cnt.astype(F32)) - p_cur[:, cs].astype(F32)
        y = jnp.dot(u.astype(BF16), pw_ref[gi], preferred_element_type=F32)
        o_ref[:, ATTN_WIDTH + gi * POOL_GROUP_W:ATTN_WIDTH + (gi + 1) * POOL_GROUP_W] = (
            y * sc_ref[:, cs]).astype(BF16)


def _mixers(sink_logits, proj, pool_w16, pool_scale):
    last_qblock = SEQ // BLOCK - 1
    last_halo = SEQ // POOL_HALO - 1
    qb_per_step = MIX_ROWS // BLOCK
    halo_per_step = MIX_ROWS // POOL_HALO
    kv_prev = lambda col: pl.BlockSpec(
        (BLOCK, KV_WIDTH), lambda n: (jnp.maximum(n * qb_per_step - 1, 0), col))
    kv_cur = lambda col: pl.BlockSpec((MIX_ROWS, KV_WIDTH), lambda n: (n, col))
    kv_next = lambda col: pl.BlockSpec(
        (BLOCK, KV_WIDTH), lambda n: (jnp.minimum((n + 1) * qb_per_step, last_qblock), col))
    return pl.pallas_call(
        _mixers_kernel,
        name="mixers",
        grid=(_MIX_STEPS,),
        in_specs=[
            pl.BlockSpec(memory_space=pltpu.SMEM),
            pl.BlockSpec((MIX_ROWS, ATTN_WIDTH), lambda n: (n, 0)),
            kv_prev(_K_COLBLOCK), kv_cur(_K_COLBLOCK), kv_next(_K_COLBLOCK),
            kv_prev(_V_COLBLOCK), kv_cur(_V_COLBLOCK), kv_next(_V_COLBLOCK),
            pl.BlockSpec((POOL_HALO, POOL_WIDTH),
                         lambda n: (jnp.maximum(n * halo_per_step - 1, 0), _P_COLBLOCK)),
            pl.BlockSpec((MIX_ROWS, POOL_WIDTH), lambda n: (n, _P_COLBLOCK)),
            pl.BlockSpec((POOL_HALO, POOL_WIDTH),
                         lambda n: (jnp.minimum((n + 1) * halo_per_step, last_halo), _P_COLBLOCK)),
            pl.BlockSpec((N_POOL_GROUPS, POOL_GROUP_W, POOL_GROUP_W), lambda n: (0, 0, 0)),
            pl.BlockSpec((1, POOL_WIDTH), lambda n: (0, 0)),
        ],
        out_specs=pl.BlockSpec((MIX_ROWS, D_MODEL), lambda n: (n, 0)),
        out_shape=jax.ShapeDtypeStruct((SEQ, D_MODEL), BF16),
        scratch_shapes=[
            pltpu.VMEM((3 * N_HEADS, BLOCK, KEYS), F32),
            pltpu.VMEM((N_POOL_GROUPS, MIX_ROWS, _MIX_EXT), BF16),
        ],
        compiler_params=_params("arbitrary"),
    )(sink_logits, proj, proj, proj, proj, proj, proj, proj, proj, proj, proj, pool_w16, pool_scale)


OUTPROJ_BM = 1024
OUTPROJ_BN = 1024
_OUTPROJ_NJ = D_MODEL // OUTPROJ_BN
_OUT_MROWS = 512


def _outproj_kernel(mx_ref, w_ref, x_ref, g2_ref, x1_ref, xg_ref, rs_ref, ss_ref):
    j = pl.program_id(1)
    @pl.when(j == 0)
    def _():
        ss_ref[...] = jnp.zeros_like(ss_ref)

    for r in range(OUTPROJ_BM // _OUT_MROWS):
        rows = slice(r * _OUT_MROWS, (r + 1) * _OUT_MROWS)
        for c in range(OUTPROJ_BN // MXU_N):
            cs = slice(c * MXU_N, (c + 1) * MXU_N)
            x1 = x_ref[rows, cs] + jnp.dot(mx_ref[rows, :], w_ref[:, cs],
                                           preferred_element_type=F32)
            x1_ref[rows, cs] = x1
            xg_ref[rows, cs] = (x1 * g2_ref[:, cs]).astype(BF16)
            sq = x1 * x1
            ss_ref[rows, :] += sum(sq[:, k * LANES:(k + 1) * LANES] for k in range(MXU_N // LANES))

    @pl.when(j == _OUTPROJ_NJ - 1)
    def _():
        ms = jnp.sum(ss_ref[...], axis=-1, keepdims=True) * (1.0 / D_MODEL)
        rs_ref[...] = lax.rsqrt(ms + RMS_EPS)


def _outproj(mixed, w_out16, x, g2):
    return pl.pallas_call(
        _outproj_kernel,
        name="outproj",
        grid=(SEQ // OUTPROJ_BM, _OUTPROJ_NJ),
        in_specs=[
            pl.BlockSpec((OUTPROJ_BM, D_MODEL), lambda i, j: (i, 0)),
            pl.BlockSpec((D_MODEL, OUTPROJ_BN), lambda i, j: (0, j)),
            pl.BlockSpec((OUTPROJ_BM, OUTPROJ_BN), lambda i, j: (i, j)),
            pl.BlockSpec((1, OUTPROJ_BN), lambda i, j: (0, j)),
        ],
        out_specs=[
            pl.BlockSpec((OUTPROJ_BM, OUTPROJ_BN), lambda i, j: (i, j)),
            pl.BlockSpec((OUTPROJ_BM, OUTPROJ_BN), lambda i, j: (i, j)),
            pl.BlockSpec((OUTPROJ_BM, 1), lambda i, j: (i, 0)),
        ],
        out_shape=[
            jax.ShapeDtypeStruct((SEQ, D_MODEL), F32),
            jax.ShapeDtypeStruct((SEQ, D_MODEL), BF16),
            jax.ShapeDtypeStruct((SEQ, 1), F32),
        ],
        scratch_shapes=[pltpu.VMEM((OUTPROJ_BM, LANES), F32)],
        compiler_params=_params("arbitrary", "arbitrary"),
    )(mixed, w_out16, x, g2)


UP_BM = 2048
_UP_MROWS = 1024
FFN_BF = 256
_N_FF_BLOCKS = D_FF // FFN_BF
DOWN_BM = 512
DOWN_BN = 512


def _ffn_up_kernel(xg_ref, rs_ref, wg_ref, wu_ref, wd_ref, a_ref, wd16_ref):
    wg = wg_ref[...].astype(BF16)
    wu = wu_ref[...].astype(BF16)
    for r in range(UP_BM // _UP_MROWS):
        rows = slice(r * _UP_MROWS, (r + 1) * _UP_MROWS)
        xg = xg_ref[rows, :]
        rs = rs_ref[rows, :]
        g = jnp.dot(xg, wg, preferred_element_type=F32) * rs
        u = jnp.dot(xg, wu, preferred_element_type=F32) * rs
        a_ref[rows, :] = (g * jax.nn.sigmoid(g) * u).astype(BF16)

    @pl.when(pl.program_id(0) == 0)
    def _():
        wd16_ref[...] = wd_ref[...].astype(BF16)


def _ffn_up(xg, rs2, w_gate, w_up, w_down):
    wd_index = lambda i, f: (jnp.where(i == 0, f, _N_FF_BLOCKS - 1), 0)
    return pl.pallas_call(
        _ffn_up_kernel,
        name="ffn_up",
        grid=(SEQ // UP_BM, _N_FF_BLOCKS),
        in_specs=[
            pl.BlockSpec((UP_BM, D_MODEL), lambda i, f: (i, 0), pipeline_mode=pl.Buffered(1)),
            pl.BlockSpec((UP_BM, 1), lambda i, f: (i, 0), pipeline_mode=pl.Buffered(1)),
            pl.BlockSpec((D_MODEL, FFN_BF), lambda i, f: (0, f)),
            pl.BlockSpec((D_MODEL, FFN_BF), lambda i, f: (0, f)),
            pl.BlockSpec((FFN_BF, D_MODEL), wd_index),
        ],
        out_specs=[
            pl.BlockSpec((UP_BM, FFN_BF), lambda i, f: (i, f)),
            pl.BlockSpec((FFN_BF, D_MODEL), wd_index),
        ],
        out_shape=[
            jax.ShapeDtypeStruct((SEQ, D_FF), BF16),
            jax.ShapeDtypeStruct((D_FF, D_MODEL), BF16),
        ],
        compiler_params=_params("arbitrary", "arbitrary"),
    )(xg, rs2, w_gate, w_up, w_down)


def _ffn_down_kernel(a_ref, w_ref, x1_ref, o_ref):
    for c in range(DOWN_BN // MXU_N):
        cs = slice(c * MXU_N, (c + 1) * MXU_N)
        o_ref[:, cs] = x1_ref[:, cs] + jnp.dot(a_ref[...], w_ref[:, cs], preferred_element_type=F32)


def _ffn_down(a, w_down16, x1):
    return pl.pallas_call(
        _ffn_down_kernel,
        name="ffn_down",
        grid=(SEQ // DOWN_BM, D_MODEL // DOWN_BN),
        in_specs=[
            pl.BlockSpec((DOWN_BM, D_FF), lambda i, j: (i, 0)),
            pl.BlockSpec((D_FF, DOWN_BN), lambda i, j: (0, j)),
            pl.BlockSpec((DOWN_BM, DOWN_BN), lambda i, j: (i, j)),
        ],
        out_specs=pl.BlockSpec((DOWN_BM, DOWN_BN), lambda i, j: (i, j)),
        out_shape=jax.ShapeDtypeStruct((SEQ, D_MODEL), F32),
        compiler_params=_params("arbitrary", "arbitrary"),
    )(a, w_down16, x1)


def kernel(x, norm1_g, w_in, q_norm_g, k_norm_g, sink_logits, pool_w, pool_scale, w_out, norm2_g,
           w_gate, w_up, w_down):
    b, s, d = x.shape
    assert (b, s, d) == (1, SEQ, D_MODEL)
    x2 = x.reshape(s, d)
    gain = jnp.concatenate([
        jnp.tile(q_norm_g, N_HEADS) * (Q_SCALE * LOG2E),
        jnp.ones((POOL_WIDTH,), F32),
        jnp.tile(k_norm_g, N_KV_HEADS),
        jnp.ones((KV_WIDTH,), F32),
    ]).reshape(1, IN_COLS)
    proj, w_out16 = _inproj(x2, norm1_g.reshape(1, D_MODEL), w_in.astype(BF16), gain, w_out)
    mixed = _mixers(sink_logits, proj, pool_w.astype(BF16), pool_scale.reshape(1, POOL_WIDTH))
    x1, xg, rs2 = _outproj(mixed, w_out16, x2, norm2_g.reshape(1, D_MODEL))
    a, w_down16 = _ffn_up(xg, rs2, w_gate, w_up, w_down)
    out = _ffn_down(a, w_down16, x1)
    return out.reshape(b, s, d)
```

```python
import jax
import jax.numpy as jnp
from jax import lax
from jax.experimental import pallas as pl
from jax.experimental.pallas import tpu as pltpu

F32 = jnp.float32
BF16 = jnp.bfloat16

D_MODEL = 4096
SEQ = 8192
HEAD_DIM = 128
N_HEADS = 16
N_KV_HEADS = 4
GROUP = N_HEADS // N_KV_HEADS
ATTN_WIDTH = N_HEADS * HEAD_DIM
KV_WIDTH = N_KV_HEADS * HEAD_DIM
POOL_WIDTH = D_MODEL - ATTN_WIDTH
POOL_WINDOWS = (2, 4, 8, 16)
N_POOL_GROUPS = len(POOL_WINDOWS)
POOL_GROUP_W = POOL_WIDTH // N_POOL_GROUPS
IN_COLS = ATTN_WIDTH + 2 * KV_WIDTH + POOL_WIDTH
D_FF = 11008
WINDOW = 128
BLOCK = 128
RMS_EPS = 1e-6
NEG_INF = -1e30
Q_SCALE = HEAD_DIM ** -0.5
LOG2E = 1.4426950408889634
ALIBI_SLOPES = tuple(2.0 ** (-8.0 * h / N_HEADS) for h in range(1, N_HEADS + 1))

VMEM_LIMIT_BYTES = 56 * 1024 * 1024
MXU_N = 256
LANES = 128
POOL_HALO = 16


def _params(*semantics):
    return pltpu.CompilerParams(dimension_semantics=semantics, vmem_limit_bytes=VMEM_LIMIT_BYTES)


INPROJ_BM = 1024
INPROJ_BN = 1024
_IN_NI = SEQ // INPROJ_BM
_IN_NJ = IN_COLS // INPROJ_BN
_IN_SUBS = 4
_IN_SUB_ROWS = INPROJ_BM // _IN_SUBS
_IN_MROWS = 512
_Q_GROUPS = ATTN_WIDTH // HEAD_DIM
_K_GROUP0 = (ATTN_WIDTH + POOL_WIDTH) // HEAD_DIM
_K_GROUPS = N_KV_HEADS


def _inproj_kernel(x_ref, g1_ref, w_ref, gain_ref, o_ref, h_a, rs_a, h_b, rs_b):
    i = pl.program_id(0)
    j = pl.program_id(1)
    row0 = pl.multiple_of(jnp.minimum(j, _IN_SUBS - 1) * _IN_SUB_ROWS, _IN_SUB_ROWS)

    def stage_rows(h_ref, rs_ref):
        x = x_ref[...]
        ms = jnp.mean(x * x, axis=-1, keepdims=True)
        rs_ref[pl.ds(row0, _IN_SUB_ROWS), :] = lax.rsqrt(ms + RMS_EPS)
        h_ref[pl.ds(row0, _IN_SUB_ROWS), :] = (x * g1_ref[...]).astype(BF16)

    def project(h_ref, rs_ref):
        for c in range(INPROJ_BN // MXU_N):
            for r in range(INPROJ_BM // _IN_MROWS):
                rows = slice(r * _IN_MROWS, (r + 1) * _IN_MROWS)
                acc = jnp.dot(h_ref[rows, :], w_ref[:, c * MXU_N:(c + 1) * MXU_N],
                              preferred_element_type=F32) * rs_ref[rows, :]
                for hh in range(MXU_N // HEAD_DIM):
                    col = c * MXU_N + hh * HEAD_DIM
                    head = j * (INPROJ_BN // HEAD_DIM) + col // HEAD_DIM
                    blk = acc[:, hh * HEAD_DIM:(hh + 1) * HEAD_DIM]
                    ms = jnp.mean(blk * blk, axis=-1, keepdims=True)
                    normed = (head < _Q_GROUPS) | ((head >= _K_GROUP0) & (head < _K_GROUP0 + _K_GROUPS))
                    scale = jnp.where(normed, lax.rsqrt(ms + RMS_EPS), 1.0)
                    o_ref[rows, col:col + HEAD_DIM] = (
                        blk * scale * gain_ref[:, col:col + HEAD_DIM]).astype(BF16)

    @pl.when(i == 0)
    def _():
        stage_rows(h_a, rs_a)

    @pl.when((i > 0) & (i % 2 == 1))
    def _():
        stage_rows(h_b, rs_b)
        project(h_a, rs_a)

    @pl.when((i > 0) & (i % 2 == 0))
    def _():
        stage_rows(h_a, rs_a)
        project(h_b, rs_b)


def _inproj(x, g1, w_in16, gain):
    assert _IN_SUBS <= _IN_NJ
    x_index = lambda i, j: (jnp.minimum(i, _IN_NI - 1) * _IN_SUBS + jnp.minimum(j, _IN_SUBS - 1), 0)
    q_blocks = ATTN_WIDTH // INPROJ_BN
    p_blocks = POOL_WIDTH // INPROJ_BN
    assert q_blocks * INPROJ_BN == ATTN_WIDTH and 2 * KV_WIDTH == INPROJ_BN
    w_col = lambda j: jnp.where(j < q_blocks, j,
                                jnp.where(j < q_blocks + p_blocks, j + 1, q_blocks))
    return pl.pallas_call(
        _inproj_kernel,
        name="inproj",
        grid=(_IN_NI + 1, _IN_NJ),
        in_specs=[
            pl.BlockSpec((_IN_SUB_ROWS, D_MODEL), x_index),
            pl.BlockSpec((1, D_MODEL), lambda i, j: (0, 0)),
            pl.BlockSpec((D_MODEL, INPROJ_BN), lambda i, j: (0, w_col(j))),
            pl.BlockSpec((1, INPROJ_BN), lambda i, j: (0, j)),
        ],
        out_specs=pl.BlockSpec((INPROJ_BM, INPROJ_BN),
                               lambda i, j: (jnp.maximum(i - 1, 0), jnp.where(i == 0, 0, j))),
        out_shape=jax.ShapeDtypeStruct((SEQ, IN_COLS), BF16),
        scratch_shapes=[
            pltpu.VMEM((INPROJ_BM, D_MODEL), BF16), pltpu.VMEM((INPROJ_BM, 1), F32),
            pltpu.VMEM((INPROJ_BM, D_MODEL), BF16), pltpu.VMEM((INPROJ_BM, 1), F32),
        ],
        compiler_params=_params("arbitrary", "arbitrary"),
    )(x, g1, w_in16, gain)


MIX_ROWS = 2 * BLOCK
_MIX_STEPS = SEQ // MIX_ROWS
_MIX_EXT = MIX_ROWS + 2 * POOL_HALO
_WO_SLAB = D_MODEL // _MIX_STEPS
KEYS = 3 * BLOCK
_P_COLBLOCK = ATTN_WIDTH // POOL_WIDTH
_K_COLBLOCK = (ATTN_WIDTH + POOL_WIDTH) // KV_WIDTH
_V_COLBLOCK = _K_COLBLOCK + 1


def _mixers_kernel(sink_ref, q_ref, kp_ref, kc_ref, kn_ref, vp_ref, vc_ref, vn_ref,
                   pp_ref, pc_ref, pn_ref, pw_ref, sc_ref, wo_ref, o_ref, wo16_ref,
                   bias_ref, band_ref):
    n = pl.program_id(0)
    wo16_ref[...] = wo_ref[...].astype(BF16)

    @pl.when(n == 0)
    def _():
        qi = lax.broadcasted_iota(jnp.int32, (BLOCK, KEYS), 0)
        kj = lax.broadcasted_iota(jnp.int32, (BLOCK, KEYS), 1)
        dist = jnp.abs(kj - BLOCK - qi)
        in_band = dist <= WINDOW
        distf = dist.astype(F32)
        for v, valid in enumerate((in_band, in_band & (kj >= BLOCK), in_band & (kj < 2 * BLOCK))):
            for h in range(N_HEADS):
                bias_ref[v * N_HEADS + h] = jnp.where(
                    valid, (-ALIBI_SLOPES[h] * LOG2E) * distf, NEG_INF)

        t = lax.broadcasted_iota(jnp.int32, (MIX_ROWS, _MIX_EXT), 0)
        s_rel = lax.broadcasted_iota(jnp.int32, (MIX_ROWS, _MIX_EXT), 1) - POOL_HALO
        for gi, w in enumerate(POOL_WINDOWS):
            left = w // 2
            right = w - 1 - left
            band_ref[gi] = jnp.where((s_rel >= t - left) & (s_rel <= t + right), 1.0, 0.0).astype(BF16)

    is_first = n == 0
    is_last = n == _MIX_STEPS - 1
    row_head = lax.broadcasted_iota(jnp.int32, (GROUP * BLOCK, 1), 0) // BLOCK
    ones_kd = jnp.ones((KEYS, HEAD_DIM), BF16)

    for qb in range(MIX_ROWS // BLOCK):
        qrows = slice(qb * BLOCK, (qb + 1) * BLOCK)
        if qb == 0:
            variant = jnp.where(is_first, 1, 0)
        else:
            variant = jnp.where(is_last, 2, 0)
        for kh in range(N_KV_HEADS):
            ksl = slice(kh * HEAD_DIM, (kh + 1) * HEAD_DIM)
            k_all = jnp.concatenate([kp_ref[:, ksl], kc_ref[:, ksl], kn_ref[:, ksl]], axis=0)
            v_all = jnp.concatenate([vp_ref[:, ksl], vc_ref[:, ksl], vn_ref[:, ksl]], axis=0)
            kw = k_all[qb * BLOCK:qb * BLOCK + KEYS]
            vw = v_all[qb * BLOCK:qb * BLOCK + KEYS]
            h0 = kh * GROUP
            qs = jnp.concatenate(
                [q_ref[qrows, (h0 + g) * HEAD_DIM:(h0 + g + 1) * HEAD_DIM] for g in range(GROUP)],
                axis=0)
            s = lax.dot_general(qs, kw, (((1,), (1,)), ((), ())), preferred_element_type=F32)
            bias = bias_ref[pl.ds(variant * N_HEADS + h0, GROUP)].reshape(GROUP * BLOCK, KEYS)
            logits = s + bias
            sink = jnp.full((GROUP * BLOCK, 1), sink_ref[h0 + GROUP - 1] * LOG2E, F32)
            for g in range(GROUP - 1):
                sink = jnp.where(row_head == g, sink_ref[h0 + g] * LOG2E, sink)
            m = jnp.maximum(jnp.max(logits, axis=-1, keepdims=True), sink)
            e = jnp.exp2(logits - m)
            ov = jnp.dot(e.astype(BF16), jnp.concatenate([vw, ones_kd], axis=1),
                         preferred_element_type=F32)
            o = ov[:, :HEAD_DIM] * (1.0 / (ov[:, HEAD_DIM:] + jnp.exp2(sink - m)))
            for g in range(GROUP):
                o_ref[qrows, (h0 + g) * HEAD_DIM:(h0 + g + 1) * HEAD_DIM] = (
                    o[g * BLOCK:(g + 1) * BLOCK].astype(BF16))

    p_cur = pc_ref[...]
    p_prev = jnp.where(is_first, jnp.zeros_like(pp_ref), pp_ref[...])
    p_next = jnp.where(is_last, jnp.zeros_like(pn_ref), pn_ref[...])
    p_ext = jnp.concatenate([p_prev, p_cur, p_next], axis=0)
    tg = n * MIX_ROWS + lax.broadcasted_iota(jnp.int32, (MIX_ROWS, 1), 0)
    for gi, w in enumerate(POOL_WINDOWS):
        left = w // 2
        right = w - 1 - left
        cs = slice(gi * POOL_GROUP_W, (gi + 1) * POOL_GROUP_W)
        win_sum = jnp.dot(band_ref[gi], p_ext[:, cs], preferred_element_type=F32)
        cnt = jnp.minimum(tg + right + 1, SEQ) - jnp.maximum(tg - left, 0)
        u = win_sum * (1.0 / cnt.astype(F32)) - p_cur[:, cs].astype(F32)
        y = jnp.dot(u.astype(BF16), pw_ref[gi], preferred_element_type=F32)
        o_ref[:, ATTN_WIDTH + gi * POOL_GROUP_W:ATTN_WIDTH + (gi + 1) * POOL_GROUP_W] = (
            y * sc_ref[:, cs]).astype(BF16)


def _mixers(sink_logits, proj, pool_w16, pool_scale, w_out):
    last_qblock = SEQ // BLOCK - 1
    last_halo = SEQ // POOL_HALO - 1
    qb_per_step = MIX_ROWS // BLOCK
    halo_per_step = MIX_ROWS // POOL_HALO
    kv_prev = lambda col: pl.BlockSpec(
        (BLOCK, KV_WIDTH), lambda n: (jnp.maximum(n * qb_per_step - 1, 0), col))
    kv_cur = lambda col: pl.BlockSpec((MIX_ROWS, KV_WIDTH), lambda n: (n, col))
    kv_next = lambda col: pl.BlockSpec(
        (BLOCK, KV_WIDTH), lambda n: (jnp.minimum((n + 1) * qb_per_step, last_qblock), col))
    return pl.pallas_call(
        _mixers_kernel,
        name="mixers",
        grid=(_MIX_STEPS,),
        in_specs=[
            pl.BlockSpec(memory_space=pltpu.SMEM),
            pl.BlockSpec((MIX_ROWS, ATTN_WIDTH), lambda n: (n, 0)),
            kv_prev(_K_COLBLOCK), kv_cur(_K_COLBLOCK), kv_next(_K_COLBLOCK),
            kv_prev(_V_COLBLOCK), kv_cur(_V_COLBLOCK), kv_next(_V_COLBLOCK),
            pl.BlockSpec((POOL_HALO, POOL_WIDTH),
                         lambda n: (jnp.maximum(n * halo_per_step - 1, 0), _P_COLBLOCK)),
            pl.BlockSpec((MIX_ROWS, POOL_WIDTH), lambda n: (n, _P_COLBLOCK)),
            pl.BlockSpec((POOL_HALO, POOL_WIDTH),
                         lambda n: (jnp.minimum((n + 1) * halo_per_step, last_halo), _P_COLBLOCK)),
            pl.BlockSpec((N_POOL_GROUPS, POOL_GROUP_W, POOL_GROUP_W), lambda n: (0, 0, 0)),
            pl.BlockSpec((1, POOL_WIDTH), lambda n: (0, 0)),
            pl.BlockSpec((_WO_SLAB, D_MODEL), lambda n: (n, 0)),
        ],
        out_specs=[
            pl.BlockSpec((MIX_ROWS, D_MODEL), lambda n: (n, 0)),
            pl.BlockSpec((_WO_SLAB, D_MODEL), lambda n: (n, 0)),
        ],
        out_shape=[
            jax.ShapeDtypeStruct((SEQ, D_MODEL), BF16),
            jax.ShapeDtypeStruct((D_MODEL, D_MODEL), BF16),
        ],
        scratch_shapes=[
            pltpu.VMEM((3 * N_HEADS, BLOCK, KEYS), F32),
            pltpu.VMEM((N_POOL_GROUPS, MIX_ROWS, _MIX_EXT), BF16),
        ],
        compiler_params=_params("arbitrary"),
    )(sink_logits, proj, proj, proj, proj, proj, proj, proj, proj, proj, proj, pool_w16, pool_scale,
      w_out)


OUTPROJ_BM = 1024
OUTPROJ_BN = 1024
_OUTPROJ_NJ = D_MODEL // OUTPROJ_BN
_OUT_MROWS = 512


def _outproj_kernel(mx_ref, w_ref, x_ref, g2_ref, x1_ref, xg_ref, rs_ref, ss_ref):
    j = pl.program_id(1)

    @pl.when(j == 0)
    def _():
        ss_ref[...] = jnp.zeros_like(ss_ref)

    for r in range(OUTPROJ_BM // _OUT_MROWS):
        rows = slice(r * _OUT_MROWS, (r + 1) * _OUT_MROWS)
        for c in range(OUTPROJ_BN // MXU_N):
            cs = slice(c * MXU_N, (c + 1) * MXU_N)
            x1 = x_ref[rows, cs] + jnp.dot(mx_ref[rows, :], w_ref[:, cs],
                                           preferred_element_type=F32)
            x1_ref[rows, cs] = x1
            xg_ref[rows, cs] = (x1 * g2_ref[:, cs]).astype(BF16)
            sq = x1 * x1
            ss_ref[rows, :] += sum(sq[:, k * LANES:(k + 1) * LANES] for k in range(MXU_N // LANES))

    @pl.when(j == _OUTPROJ_NJ - 1)
    def _():
        ms = jnp.sum(ss_ref[...], axis=-1, keepdims=True) * (1.0 / D_MODEL)
        rs_ref[...] = lax.rsqrt(ms + RMS_EPS)


def _outproj(mixed, w_out16, x, g2):
    return pl.pallas_call(
        _outproj_kernel,
        name="outproj",
        grid=(SEQ // OUTPROJ_BM, _OUTPROJ_NJ),
        in_specs=[
            pl.BlockSpec((OUTPROJ_BM, D_MODEL), lambda i, j: (i, 0)),
            pl.BlockSpec((D_MODEL, OUTPROJ_BN), lambda i, j: (0, j)),
            pl.BlockSpec((OUTPROJ_BM, OUTPROJ_BN), lambda i, j: (i, j)),
            pl.BlockSpec((1, OUTPROJ_BN), lambda i, j: (0, j)),
        ],
        out_specs=[
            pl.BlockSpec((OUTPROJ_BM, OUTPROJ_BN), lambda i, j: (i, j)),
            pl.BlockSpec((OUTPROJ_BM, OUTPROJ_BN), lambda i, j: (i, j)),
            pl.BlockSpec((OUTPROJ_BM, 1), lambda i, j: (i, 0)),
        ],
        out_shape=[
            jax.ShapeDtypeStruct((SEQ, D_MODEL), F32),
            jax.ShapeDtypeStruct((SEQ, D_MODEL), BF16),
            jax.ShapeDtypeStruct((SEQ, 1), F32),
        ],
        scratch_shapes=[pltpu.VMEM((OUTPROJ_BM, LANES), F32)],
        compiler_params=_params("arbitrary", "arbitrary"),
    )(mixed, w_out16, x, g2)


UP_BM = 2048
_UP_MROWS = 1024
FFN_BF = 256
_N_FF_BLOCKS = D_FF // FFN_BF
DOWN_BM = 512
DOWN_BN = 512


def _ffn_up_kernel(xg_ref, rs_ref, wg_ref, wu_ref, wd_ref, a_ref, wd16_ref):
    wg = wg_ref[...].astype(BF16)
    wu = wu_ref[...].astype(BF16)
    for r in range(UP_BM // _UP_MROWS):
        rows = slice(r * _UP_MROWS, (r + 1) * _UP_MROWS)
        xg = xg_ref[rows, :]
        rs = rs_ref[rows, :]
        g = jnp.dot(xg, wg, preferred_element_type=F32) * rs
        u = jnp.dot(xg, wu, preferred_element_type=F32) * rs
        a_ref[rows, :] = (g * jax.nn.sigmoid(g) * u).astype(BF16)

    @pl.when(pl.program_id(0) == 0)
    def _():
        wd16_ref[...] = wd_ref[...].astype(BF16)


def _ffn_up(xg, rs2, w_gate, w_up, w_down):
    wd_index = lambda i, f: (jnp.where(i == 0, f, _N_FF_BLOCKS - 1), 0)
    return pl.pallas_call(
        _ffn_up_kernel,
        name="ffn_up",
        grid=(SEQ // UP_BM, _N_FF_BLOCKS),
        in_specs=[
            pl.BlockSpec((UP_BM, D_MODEL), lambda i, f: (i, 0), pipeline_mode=pl.Buffered(1)),
            pl.BlockSpec((UP_BM, 1), lambda i, f: (i, 0), pipeline_mode=pl.Buffered(1)),
            pl.BlockSpec((D_MODEL, FFN_BF), lambda i, f: (0, f)),
            pl.BlockSpec((D_MODEL, FFN_BF), lambda i, f: (0, f)),
            pl.BlockSpec((FFN_BF, D_MODEL), wd_index),
        ],
        out_specs=[
            pl.BlockSpec((UP_BM, FFN_BF), lambda i, f: (i, f)),
            pl.BlockSpec((FFN_BF, D_MODEL), wd_index),
        ],
        out_shape=[
            jax.ShapeDtypeStruct((SEQ, D_FF), BF16),
            jax.ShapeDtypeStruct((D_FF, D_MODEL), BF16),
        ],
        compiler_params=_params("arbitrary", "arbitrary"),
    )(xg, rs2, w_gate, w_up, w_down)


def _ffn_down_kernel(a_ref, w_ref, x1_ref, o_ref):
    for c in range(DOWN_BN // MXU_N):
        cs = slice(c * MXU_N, (c + 1) * MXU_N)
        o_ref[:, cs] = x1_ref[:, cs] + jnp.dot(a_ref[...], w_ref[:, cs], preferred_element_type=F32)


def _ffn_down(a, w_down16, x1):
    return pl.pallas_call(
        _ffn_down_kernel,
        name="ffn_down",
        grid=(SEQ // DOWN_BM, D_MODEL // DOWN_BN),
        in_specs=[
            pl.BlockSpec((DOWN_BM, D_FF), lambda i, j: (i, 0)),
            pl.BlockSpec((D_FF, DOWN_BN), lambda i, j: (0, j)),
            pl.BlockSpec((DOWN_BM, DOWN_BN), lambda i, j: (i, j)),
        ],
        out_specs=pl.BlockSpec((DOWN_BM, DOWN_BN), lambda i, j: (i, j)),
        out_shape=jax.ShapeDtypeStruct((SEQ, D_MODEL), F32),
        compiler_params=_params("arbitrary", "arbitrary"),
    )(a, w_down16, x1)


def kernel(x, norm1_g, w_in, q_norm_g, k_norm_g, sink_logits, pool_w, pool_scale, w_out, norm2_g,
           w_gate, w_up, w_down):
    b, s, d = x.shape
    assert (b, s, d) == (1, SEQ, D_MODEL)
    x2 = x.reshape(s, d)
    gain = jnp.concatenate([
        jnp.tile(q_norm_g, N_HEADS) * (Q_SCALE * LOG2E),
        jnp.ones((POOL_WIDTH,), F32),
        jnp.tile(k_norm_g, N_KV_HEADS),
        jnp.ones((KV_WIDTH,), F32),
    ]).reshape(1, IN_COLS)
    proj = _inproj(x2, norm1_g.reshape(1, D_MODEL), w_in.astype(BF16), gain)
    mixed, w_out16 = _mixers(sink_logits, proj, pool_w.astype(BF16),
                             pool_scale.reshape(1, POOL_WIDTH), w_out)
    x1, xg, rs2 = _outproj(mixed, w_out16, x2, norm2_g.reshape(1, D_MODEL))
    a, w_down16 = _ffn_up(xg, rs2, w_gate, w_up, w_down)
    out = _ffn_down(a, w_down16, x1)
    return out.reshape(b, s, d)
```

```python
import jax
import jax.numpy as jnp
from jax import lax
from jax.experimental import pallas as pl
from jax.experimental.pallas import tpu as pltpu

F32 = jnp.float32
BF16 = jnp.bfloat16

D_MODEL = 4096
SEQ = 8192
HEAD_DIM = 128
N_HEADS = 16
N_KV_HEADS = 4
GROUP = N_HEADS // N_KV_HEADS
ATTN_WIDTH = N_HEADS * HEAD_DIM
KV_WIDTH = N_KV_HEADS * HEAD_DIM
POOL_WIDTH = D_MODEL - ATTN_WIDTH
POOL_WINDOWS = (2, 4, 8, 16)
N_POOL_GROUPS = len(POOL_WINDOWS)
POOL_GROUP_W = POOL_WIDTH // N_POOL_GROUPS
IN_COLS = ATTN_WIDTH + 2 * KV_WIDTH + POOL_WIDTH
D_FF = 11008
WINDOW = 128
BLOCK = 128
RMS_EPS = 1e-6
NEG_INF = -1e30
Q_SCALE = HEAD_DIM ** -0.5
LOG2E = 1.4426950408889634
ALIBI_SLOPES = tuple(2.0 ** (-8.0 * h / N_HEADS) for h in range(1, N_HEADS + 1))

VMEM_LIMIT_BYTES = 56 * 1024 * 1024
MXU_N = 256
LANES = 128
POOL_HALO = 16


def _params(*semantics):
    return pltpu.CompilerParams(dimension_semantics=semantics, vmem_limit_bytes=VMEM_LIMIT_BYTES)


INPROJ_BM = 1024
INPROJ_BN = 1024
_IN_NI = SEQ // INPROJ_BM
_IN_NJ = IN_COLS // INPROJ_BN
_IN_SUBS = 4
_IN_SUB_ROWS = INPROJ_BM // _IN_SUBS
_IN_MROWS = 512
_IN_STAGE_COLS = 512
_Q_GROUPS = ATTN_WIDTH // HEAD_DIM
_K_GROUP0 = (ATTN_WIDTH + POOL_WIDTH) // HEAD_DIM
_K_GROUPS = N_KV_HEADS


def _inproj_kernel(x_ref, g1_ref, w_ref, gain_ref, o_ref, h_a, rs_a, h_b, rs_b):
    i = pl.program_id(0)
    j = pl.program_id(1)
    row0 = pl.multiple_of(jnp.minimum(j, _IN_SUBS - 1) * _IN_SUB_ROWS, _IN_SUB_ROWS)

    def stage_rows(h_ref, rs_ref):
        sq_lanes = jnp.zeros((_IN_SUB_ROWS, LANES), F32)
        for k in range(D_MODEL // _IN_STAGE_COLS):
            cs = slice(k * _IN_STAGE_COLS, (k + 1) * _IN_STAGE_COLS)
            x = x_ref[:, cs]
            sq = x * x
            sq_lanes = sq_lanes + sum(
                sq[:, t * LANES:(t + 1) * LANES] for t in range(_IN_STAGE_COLS // LANES))
            h_ref[pl.ds(row0, _IN_SUB_ROWS), cs] = (x * g1_ref[:, cs]).astype(BF16)
        ms = jnp.sum(sq_lanes, axis=-1, keepdims=True) * (1.0 / D_MODEL)
        rs_ref[pl.ds(row0, _IN_SUB_ROWS), :] = lax.rsqrt(ms + RMS_EPS)

    def project(h_ref, rs_ref):
        for c in range(INPROJ_BN // MXU_N):
            for r in range(INPROJ_BM // _IN_MROWS):
                rows = slice(r * _IN_MROWS, (r + 1) * _IN_MROWS)
                acc = jnp.dot(h_ref[rows, :], w_ref[:, c * MXU_N:(c + 1) * MXU_N],
                              preferred_element_type=F32) * rs_ref[rows, :]
                for hh in range(MXU_N // HEAD_DIM):
                    col = c * MXU_N + hh * HEAD_DIM
                    head = j * (INPROJ_BN // HEAD_DIM) + col // HEAD_DIM
                    blk = acc[:, hh * HEAD_DIM:(hh + 1) * HEAD_DIM]
                    ms = jnp.mean(blk * blk, axis=-1, keepdims=True)
                    normed = (head < _Q_GROUPS) | ((head >= _K_GROUP0) & (head < _K_GROUP0 + _K_GROUPS))
                    scale = jnp.where(normed, lax.rsqrt(ms + RMS_EPS), 1.0)
                    o_ref[rows, col:col + HEAD_DIM] = (
                        blk * scale * gain_ref[:, col:col + HEAD_DIM]).astype(BF16)

    @pl.when(i == 0)
    def _():
        stage_rows(h_a, rs_a)

    @pl.when((i > 0) & (i % 2 == 1))
    def _():
        stage_rows(h_b, rs_b)
        project(h_a, rs_a)

    @pl.when((i > 0) & (i % 2 == 0))
    def _():
        stage_rows(h_a, rs_a)
        project(h_b, rs_b)


def _inproj(x, g1, w_in16, gain):
    assert _IN_SUBS <= _IN_NJ
    x_index = lambda i, j: (jnp.minimum(i, _IN_NI - 1) * _IN_SUBS + jnp.minimum(j, _IN_SUBS - 1), 0)
    q_blocks = ATTN_WIDTH // INPROJ_BN
    p_blocks = POOL_WIDTH // INPROJ_BN
    assert q_blocks * INPROJ_BN == ATTN_WIDTH and 2 * KV_WIDTH == INPROJ_BN
    w_col = lambda j: jnp.where(j < q_blocks, j,
                                jnp.where(j < q_blocks + p_blocks, j + 1, q_blocks))
    return pl.pallas_call(
        _inproj_kernel,
        name="inproj",
        grid=(_IN_NI + 1, _IN_NJ),
        in_specs=[
            pl.BlockSpec((_IN_SUB_ROWS, D_MODEL), x_index),
            pl.BlockSpec((1, D_MODEL), lambda i, j: (0, 0)),
            pl.BlockSpec((D_MODEL, INPROJ_BN), lambda i, j: (0, w_col(j))),
            pl.BlockSpec((1, INPROJ_BN), lambda i, j: (0, j)),
        ],
        out_specs=pl.BlockSpec((INPROJ_BM, INPROJ_BN),
                               lambda i, j: (jnp.maximum(i - 1, 0), jnp.where(i == 0, 0, j))),
        out_shape=jax.ShapeDtypeStruct((SEQ, IN_COLS), BF16),
        scratch_shapes=[
            pltpu.VMEM((INPROJ_BM, D_MODEL), BF16), pltpu.VMEM((INPROJ_BM, 1), F32),
            pltpu.VMEM((INPROJ_BM, D_MODEL), BF16), pltpu.VMEM((INPROJ_BM, 1), F32),
        ],
        compiler_params=_params("arbitrary", "arbitrary"),
    )(x, g1, w_in16, gain)


MIX_ROWS = 2 * BLOCK
_MIX_STEPS = SEQ // MIX_ROWS
_MIX_EXT = MIX_ROWS + 2 * POOL_HALO
_WO_SLAB = D_MODEL // _MIX_STEPS
KEYS = 3 * BLOCK
_P_COLBLOCK = ATTN_WIDTH // POOL_WIDTH
_K_COLBLOCK = (ATTN_WIDTH + POOL_WIDTH) // KV_WIDTH
_V_COLBLOCK = _K_COLBLOCK + 1


def _mixers_kernel(sink_ref, q_ref, kp_ref, kc_ref, kn_ref, vp_ref, vc_ref, vn_ref,
                   pp_ref, pc_ref, pn_ref, pw_ref, sc_ref, wo_ref, o_ref, wo16_ref,
                   bias_ref, band_ref):
    n = pl.program_id(0)
    wo16_ref[...] = wo_ref[...].astype(BF16)

    @pl.when(n == 0)
    def _():
        qi = lax.broadcasted_iota(jnp.int32, (BLOCK, KEYS), 0)
        kj = lax.broadcasted_iota(jnp.int32, (BLOCK, KEYS), 1)
        dist = jnp.abs(kj - BLOCK - qi)
        in_band = dist <= WINDOW
        distf = dist.astype(F32)
        for v, valid in enumerate((in_band, in_band & (kj >= BLOCK), in_band & (kj < 2 * BLOCK))):
            for h in range(N_HEADS):
                bias_ref[v * N_HEADS + h] = jnp.where(
                    valid, (-ALIBI_SLOPES[h] * LOG2E) * distf, NEG_INF)

        t = lax.broadcasted_iota(jnp.int32, (MIX_ROWS, _MIX_EXT), 0)
        s_rel = lax.broadcasted_iota(jnp.int32, (MIX_ROWS, _MIX_EXT), 1) - POOL_HALO
        for gi, w in enumerate(POOL_WINDOWS):
            left = w // 2
            right = w - 1 - left
            band_ref[gi] = jnp.where((s_rel >= t - left) & (s_rel <= t + right), 1.0, 0.0).astype(BF16)

    is_first = n == 0
    is_last = n == _MIX_STEPS - 1
    row_head = lax.broadcasted_iota(jnp.int32, (GROUP * BLOCK, 1), 0) // BLOCK
    ones_kd = jnp.ones((KEYS, HEAD_DIM), BF16)

    p_cur = pc_ref[...]
    p_prev = jnp.where(is_first, jnp.zeros_like(pp_ref), pp_ref[...])
    p_next = jnp.where(is_last, jnp.zeros_like(pn_ref), pn_ref[...])
    p_ext = jnp.concatenate([p_prev, p_cur, p_next], axis=0)
    tg = n * MIX_ROWS + lax.broadcasted_iota(jnp.int32, (MIX_ROWS, 1), 0)

    def pool_group(gi):
        w = POOL_WINDOWS[gi]
        left = w // 2
        right = w - 1 - left
        cs = slice(gi * POOL_GROUP_W, (gi + 1) * POOL_GROUP_W)
        win_sum = jnp.dot(band_ref[gi], p_ext[:, cs], preferred_element_type=F32)
        cnt = jnp.minimum(tg + right + 1, SEQ) - jnp.maximum(tg - left, 0)
        u = win_sum * (1.0 / cnt.astype(F32)) - p_cur[:, cs].astype(F32)
        y = jnp.dot(u.astype(BF16), pw_ref[gi], preferred_element_type=F32)
        o_ref[:, ATTN_WIDTH + gi * POOL_GROUP_W:ATTN_WIDTH + (gi + 1) * POOL_GROUP_W] = (
            y * sc_ref[:, cs]).astype(BF16)

    assert N_POOL_GROUPS == N_KV_HEADS
    for kh in range(N_KV_HEADS):
        ksl = slice(kh * HEAD_DIM, (kh + 1) * HEAD_DIM)
        k_all = jnp.concatenate([kp_ref[:, ksl], kc_ref[:, ksl], kn_ref[:, ksl]], axis=0)
        v_all = jnp.concatenate([vp_ref[:, ksl], vc_ref[:, ksl], vn_ref[:, ksl]], axis=0)
        h0 = kh * GROUP
        sink = jnp.full((GROUP * BLOCK, 1), sink_ref[h0 + GROUP - 1] * LOG2E, F32)
        for g in range(GROUP - 1):
            sink = jnp.where(row_head == g, sink_ref[h0 + g] * LOG2E, sink)
        for qb in range(MIX_ROWS // BLOCK):
            qrows = slice(qb * BLOCK, (qb + 1) * BLOCK)
            variant = jnp.where(is_first, 1, 0) if qb == 0 else jnp.where(is_last, 2, 0)
            kw = k_all[qb * BLOCK:qb * BLOCK + KEYS]
            vw = v_all[qb * BLOCK:qb * BLOCK + KEYS]
            qs = jnp.concatenate(
                [q_ref[qrows, (h0 + g) * HEAD_DIM:(h0 + g + 1) * HEAD_DIM] for g in range(GROUP)],
                axis=0)
            s = lax.dot_general(qs, kw, (((1,), (1,)), ((), ())), preferred_element_type=F32)
            bias = bias_ref[pl.ds(variant * N_HEADS + h0, GROUP)].reshape(GROUP * BLOCK, KEYS)
            logits = s + bias
            m = jnp.maximum(jnp.max(logits, axis=-1, keepdims=True), sink)
            e = jnp.exp2(logits - m)
            ov = jnp.dot(e.astype(BF16), jnp.concatenate([vw, ones_kd], axis=1),
                         preferred_element_type=F32)
            o = ov[:, :HEAD_DIM] * (1.0 / (ov[:, HEAD_DIM:] + jnp.exp2(sink - m)))
            for g in range(GROUP):
                o_ref[qrows, (h0 + g) * HEAD_DIM:(h0 + g + 1) * HEAD_DIM] = (
                    o[g * BLOCK:(g + 1) * BLOCK].astype(BF16))
        pool_group(kh)


def _mixers(sink_logits, proj, pool_w16, pool_scale, w_out):
    last_qblock = SEQ // BLOCK - 1
    last_halo = SEQ // POOL_HALO - 1
    qb_per_step = MIX_ROWS // BLOCK
    halo_per_step = MIX_ROWS // POOL_HALO
    kv_prev = lambda col: pl.BlockSpec(
        (BLOCK, KV_WIDTH), lambda n: (jnp.maximum(n * qb_per_step - 1, 0), col))
    kv_cur = lambda col: pl.BlockSpec((MIX_ROWS, KV_WIDTH), lambda n: (n, col))
    kv_next = lambda col: pl.BlockSpec(
        (BLOCK, KV_WIDTH), lambda n: (jnp.minimum((n + 1) * qb_per_step, last_qblock), col))
    return pl.pallas_call(
        _mixers_kernel,
        name="mixers",
        grid=(_MIX_STEPS,),
        in_specs=[
            pl.BlockSpec(memory_space=pltpu.SMEM),
            pl.BlockSpec((MIX_ROWS, ATTN_WIDTH), lambda n: (n, 0)),
            kv_prev(_K_COLBLOCK), kv_cur(_K_COLBLOCK), kv_next(_K_COLBLOCK),
            kv_prev(_V_COLBLOCK), kv_cur(_V_COLBLOCK), kv_next(_V_COLBLOCK),
            pl.BlockSpec((POOL_HALO, POOL_WIDTH),
                         lambda n: (jnp.maximum(n * halo_per_step - 1, 0), _P_COLBLOCK)),
            pl.BlockSpec((MIX_ROWS, POOL_WIDTH), lambda n: (n, _P_COLBLOCK)),
            pl.BlockSpec((POOL_HALO, POOL_WIDTH),
                         lambda n: (jnp.minimum((n + 1) * halo_per_step, last_halo), _P_COLBLOCK)),
            pl.BlockSpec((N_POOL_GROUPS, POOL_GROUP_W, POOL_GROUP_W), lambda n: (0, 0, 0)),
            pl.BlockSpec((1, POOL_WIDTH), lambda n: (0, 0)),
            pl.BlockSpec((_WO_SLAB, D_MODEL), lambda n: (n, 0)),
        ],
        out_specs=[
            pl.BlockSpec((MIX_ROWS, D_MODEL), lambda n: (n, 0)),
            pl.BlockSpec((_WO_SLAB, D_MODEL), lambda n: (n, 0)),
        ],
        out_shape=[
            jax.ShapeDtypeStruct((SEQ, D_MODEL), BF16),
            jax.ShapeDtypeStruct((D_MODEL, D_MODEL), BF16),
        ],
        scratch_shapes=[
            pltpu.VMEM((3 * N_HEADS, BLOCK, KEYS), F32),
            pltpu.VMEM((N_POOL_GROUPS, MIX_ROWS, _MIX_EXT), BF16),
        ],
        compiler_params=_params("arbitrary"),
    )(sink_logits, proj, proj, proj, proj, proj, proj, proj, proj, proj, proj, pool_w16, pool_scale,
      w_out)


OUTPROJ_BM = 1024
OUTPROJ_BN = 1024
_OUTPROJ_NJ = D_MODEL // OUTPROJ_BN
_OUT_MROWS = 512


def _outproj_kernel(mx_ref, w_ref, x_ref, g2_ref, x1_ref, xg_ref, rs_ref, ss_ref):
    j = pl.program_id(1)

    @pl.when(j == 0)
    def _():
        ss_ref[...] = jnp.zeros_like(ss_ref)

    for r in range(OUTPROJ_BM // _OUT_MROWS):
        rows = slice(r * _OUT_MROWS, (r + 1) * _OUT_MROWS)
        for c in range(OUTPROJ_BN // MXU_N):
            cs = slice(c * MXU_N, (c + 1) * MXU_N)
            x1 = x_ref[rows, cs] + jnp.dot(mx_ref[rows, :], w_ref[:, cs],
                                           preferred_element_type=F32)
            x1_ref[rows, cs] = x1
            xg_ref[rows, cs] = (x1 * g2_ref[:, cs]).astype(BF16)
            sq = x1 * x1
            ss_ref[rows, :] += sum(sq[:, k * LANES:(k + 1) * LANES] for k in range(MXU_N // LANES))

    @pl.when(j == _OUTPROJ_NJ - 1)
    def _():
        ms = jnp.sum(ss_ref[...], axis=-1, keepdims=True) * (1.0 / D_MODEL)
        rs_ref[...] = lax.rsqrt(ms + RMS_EPS)


def _outproj(mixed, w_out16, x, g2):
    return pl.pallas_call(
        _outproj_kernel,
        name="outproj",
        grid=(SEQ // OUTPROJ_BM, _OUTPROJ_NJ),
        in_specs=[
            pl.BlockSpec((OUTPROJ_BM, D_MODEL), lambda i, j: (i, 0)),
            pl.BlockSpec((D_MODEL, OUTPROJ_BN), lambda i, j: (0, j)),
            pl.BlockSpec((OUTPROJ_BM, OUTPROJ_BN), lambda i, j: (i, j)),
            pl.BlockSpec((1, OUTPROJ_BN), lambda i, j: (0, j)),
        ],
        out_specs=[
            pl.BlockSpec((OUTPROJ_BM, OUTPROJ_BN), lambda i, j: (i, j)),
            pl.BlockSpec((OUTPROJ_BM, OUTPROJ_BN), lambda i, j: (i, j)),
            pl.BlockSpec((OUTPROJ_BM, 1), lambda i, j: (i, 0)),
        ],
        out_shape=[
            jax.ShapeDtypeStruct((SEQ, D_MODEL), F32),
            jax.ShapeDtypeStruct((SEQ, D_MODEL), BF16),
            jax.ShapeDtypeStruct((SEQ, 1), F32),
        ],
        scratch_shapes=[pltpu.VMEM((OUTPROJ_BM, LANES), F32)],
        compiler_params=_params("arbitrary", "arbitrary"),
    )(mixed, w_out16, x, g2)


UP_BM = 2048
_UP_MROWS = 1024
FFN_BF = 256
_N_FF_BLOCKS = D_FF // FFN_BF
DOWN_BM = 512
DOWN_BN = 512
_DOWN_K0 = (0, 2816, 5632, 8448)
_DOWN_KLEN = (2816, 2816, 2816, 2560)
_DOWN_KCHUNKS = len(_DOWN_K0)


def _ffn_up_kernel(xg_ref, rs_ref, wg_ref, wu_ref, wd_ref, a_ref, wd16_ref):
    wg = wg_ref[...].astype(BF16)
    wu = wu_ref[...].astype(BF16)
    for r in range(UP_BM // _UP_MROWS):
        rows = slice(r * _UP_MROWS, (r + 1) * _UP_MROWS)
        xg = xg_ref[rows, :]
        rs = rs_ref[rows, :]
        g = jnp.dot(xg, wg, preferred_element_type=F32) * rs
        u = jnp.dot(xg, wu, preferred_element_type=F32) * rs
        a_ref[rows, :] = (g * jax.nn.sigmoid(g) * u).astype(BF16)

    @pl.when(pl.program_id(0) == 0)
    def _():
        wd16_ref[...] = wd_ref[...].astype(BF16)


def _ffn_up(xg, rs2, w_gate, w_up, w_down):
    wd_index = lambda i, f: (jnp.where(i == 0, f, _N_FF_BLOCKS - 1), 0)
    return pl.pallas_call(
        _ffn_up_kernel,
        name="ffn_up",
        grid=(SEQ // UP_BM, _N_FF_BLOCKS),
        in_specs=[
            pl.BlockSpec((UP_BM, D_MODEL), lambda i, f: (i, 0), pipeline_mode=pl.Buffered(1)),
            pl.BlockSpec((UP_BM, 1), lambda i, f: (i, 0), pipeline_mode=pl.Buffered(1)),
            pl.BlockSpec((D_MODEL, FFN_BF), lambda i, f: (0, f)),
            pl.BlockSpec((D_MODEL, FFN_BF), lambda i, f: (0, f)),
            pl.BlockSpec((FFN_BF, D_MODEL), wd_index),
        ],
        out_specs=[
            pl.BlockSpec((UP_BM, FFN_BF), lambda i, f: (i, f)),
            pl.BlockSpec((FFN_BF, D_MODEL), wd_index),
        ],
        out_shape=[
            jax.ShapeDtypeStruct((SEQ, D_FF), BF16),
            jax.ShapeDtypeStruct((D_FF, D_MODEL), BF16),
        ],
        compiler_params=_params("arbitrary", "arbitrary"),
    )(xg, rs2, w_gate, w_up, w_down)


def _ffn_down_kernel(a_hbm, w_ref, x1_ref, o_ref, a_buf, sems):
    i = pl.program_id(0)
    j = pl.program_id(1)
    n_i = pl.num_programs(0)
    n_j = pl.num_programs(1)

    def chunk_copy(row_block, c):
        k0, kl = _DOWN_K0[c], _DOWN_KLEN[c]
        return pltpu.make_async_copy(
            a_hbm.at[pl.ds(row_block * DOWN_BM, DOWN_BM), pl.ds(k0, kl)],
            a_buf.at[row_block % 2, :, pl.ds(k0, kl)], sems.at[c])

    @pl.when((i == 0) & (j == 0))
    def _():
        for c in range(_DOWN_KCHUNKS):
            chunk_copy(0, c).start()

    @pl.when(j == 0)
    def _():
        for c in range(_DOWN_KCHUNKS):
            chunk_copy(i, c).wait()

    for c in range(_DOWN_KCHUNKS):
        @pl.when((j == n_j - _DOWN_KCHUNKS + c) & (i < n_i - 1))
        def _():
            chunk_copy(i + 1, c).start()

    a_blk = a_buf.at[i % 2]
    for c in range(DOWN_BN // MXU_N):
        cs = slice(c * MXU_N, (c + 1) * MXU_N)
        o_ref[:, cs] = x1_ref[:, cs] + jnp.dot(a_blk[...], w_ref[:, cs], preferred_element_type=F32)


def _ffn_down(a, w_down16, x1):
    assert _DOWN_KCHUNKS <= D_MODEL // DOWN_BN
    return pl.pallas_call(
        _ffn_down_kernel,
        name="ffn_down",
        grid=(SEQ // DOWN_BM, D_MODEL // DOWN_BN),
        in_specs=[
            pl.BlockSpec(memory_space=pl.ANY),
            pl.BlockSpec((D_FF, DOWN_BN), lambda i, j: (0, j)),
            pl.BlockSpec((DOWN_BM, DOWN_BN), lambda i, j: (i, j)),
        ],
        out_specs=pl.BlockSpec((DOWN_BM, DOWN_BN), lambda i, j: (i, j)),
        out_shape=jax.ShapeDtypeStruct((SEQ, D_MODEL), F32),
        scratch_shapes=[
            pltpu.VMEM((2, DOWN_BM, D_FF), BF16),
            pltpu.SemaphoreType.DMA((_DOWN_KCHUNKS,)),
        ],
        compiler_params=_params("arbitrary", "arbitrary"),
    )(a, w_down16, x1)


def kernel(x, norm1_g, w_in, q_norm_g, k_norm_g, sink_logits, pool_w, pool_scale, w_out, norm2_g,
           w_gate, w_up, w_down):
    b, s, d = x.shape
    assert (b, s, d) == (1, SEQ, D_MODEL)
    x2 = x.reshape(s, d)
    gain = jnp.concatenate([
        jnp.tile(q_norm_g, N_HEADS) * (Q_SCALE * LOG2E),
        jnp.ones((POOL_WIDTH,), F32),
        jnp.tile(k_norm_g, N_KV_HEADS),
        jnp.ones((KV_WIDTH,), F32),
    ]).reshape(1, IN_COLS)
    proj = _inproj(x2, norm1_g.reshape(1, D_MODEL), w_in.astype(BF16), gain)
    mixed, w_out16 = _mixers(sink_logits, proj, pool_w.astype(BF16),
                             pool_scale.reshape(1, POOL_WIDTH), w_out)
    x1, xg, rs2 = _outproj(mixed, w_out16, x2, norm2_g.reshape(1, D_MODEL))
    a, w_down16 = _ffn_up(xg, rs2, w_gate, w_up, w_down)
    out = _ffn_down(a, w_down16, x1)
    return out.reshape(b, s, d)
```

```python
import jax
import jax.numpy as jnp
from jax import lax
from jax.experimental import pallas as pl
from jax.experimental.pallas import tpu as pltpu

F32 = jnp.float32
BF16 = jnp.bfloat16

D_MODEL = 4096
SEQ = 8192
HEAD_DIM = 128
N_HEADS = 16
N_KV_HEADS = 4
GROUP = N_HEADS // N_KV_HEADS
ATTN_WIDTH = N_HEADS * HEAD_DIM
KV_WIDTH = N_KV_HEADS * HEAD_DIM
POOL_WIDTH = D_MODEL - ATTN_WIDTH
POOL_WINDOWS = (2, 4, 8, 16)
N_POOL_GROUPS = len(POOL_WINDOWS)
POOL_GROUP_W = POOL_WIDTH // N_POOL_GROUPS
IN_COLS = ATTN_WIDTH + 2 * KV_WIDTH + POOL_WIDTH
D_FF = 11008
WINDOW = 128
BLOCK = 128
RMS_EPS = 1e-6
NEG_INF = -1e30
Q_SCALE = HEAD_DIM ** -0.5
LOG2E = 1.4426950408889634
ALIBI_SLOPES = tuple(2.0 ** (-8.0 * h / N_HEADS) for h in range(1, N_HEADS + 1))

VMEM_LIMIT_BYTES = 56 * 1024 * 1024
MXU_N = 256
LANES = 128
POOL_HALO = 16


def _params(*semantics):
    return pltpu.CompilerParams(dimension_semantics=semantics, vmem_limit_bytes=VMEM_LIMIT_BYTES)


INPROJ_BM = 1024
INPROJ_BN = 1024
_IN_NI = SEQ // INPROJ_BM
_IN_NJ = IN_COLS // INPROJ_BN
_IN_SUBS = 4
_IN_SUB_ROWS = INPROJ_BM // _IN_SUBS
_IN_MROWS = 512
_IN_STAGE_COLS = 512
_Q_GROUPS = ATTN_WIDTH // HEAD_DIM
_K_GROUP0 = (ATTN_WIDTH + POOL_WIDTH) // HEAD_DIM
_K_GROUPS = N_KV_HEADS


def _inproj_kernel(x_ref, g1_ref, w_ref, gain_ref, o_ref, h_a, rs_a, h_b, rs_b):
    i = pl.program_id(0)
    j = pl.program_id(1)
    row0 = pl.multiple_of(jnp.minimum(j, _IN_SUBS - 1) * _IN_SUB_ROWS, _IN_SUB_ROWS)

    def stage_rows(h_ref, rs_ref):
        sq_lanes = jnp.zeros((_IN_SUB_ROWS, LANES), F32)
        for k in range(D_MODEL // _IN_STAGE_COLS):
            cs = slice(k * _IN_STAGE_COLS, (k + 1) * _IN_STAGE_COLS)
            x = x_ref[:, cs]
            sq = x * x
            sq_lanes = sq_lanes + sum(
                sq[:, t * LANES:(t + 1) * LANES] for t in range(_IN_STAGE_COLS // LANES))
            h_ref[pl.ds(row0, _IN_SUB_ROWS), cs] = (x * g1_ref[:, cs]).astype(BF16)
        ms = jnp.sum(sq_lanes, axis=-1, keepdims=True) * (1.0 / D_MODEL)
        rs_ref[pl.ds(row0, _IN_SUB_ROWS), :] = lax.rsqrt(ms + RMS_EPS)

    def project(h_ref, rs_ref):
        for c in range(INPROJ_BN // MXU_N):
            for r in range(INPROJ_BM // _IN_MROWS):
                rows = slice(r * _IN_MROWS, (r + 1) * _IN_MROWS)
                acc = jnp.dot(h_ref[rows, :], w_ref[:, c * MXU_N:(c + 1) * MXU_N],
                              preferred_element_type=F32) * rs_ref[rows, :]
                for hh in range(MXU_N // HEAD_DIM):
                    col = c * MXU_N + hh * HEAD_DIM
                    head = j * (INPROJ_BN // HEAD_DIM) + col // HEAD_DIM
                    blk = acc[:, hh * HEAD_DIM:(hh + 1) * HEAD_DIM]
                    ms = jnp.mean(blk * blk, axis=-1, keepdims=True)
                    normed = (head < _Q_GROUPS) | ((head >= _K_GROUP0) & (head < _K_GROUP0 + _K_GROUPS))
                    scale = jnp.where(normed, lax.rsqrt(ms + RMS_EPS), 1.0)
                    o_ref[rows, col:col + HEAD_DIM] = (
                        blk * scale * gain_ref[:, col:col + HEAD_DIM]).astype(BF16)

    @pl.when(i == 0)
    def _():
        stage_rows(h_a, rs_a)

    @pl.when((i > 0) & (i % 2 == 1))
    def _():
        stage_rows(h_b, rs_b)
        project(h_a, rs_a)

    @pl.when((i > 0) & (i % 2 == 0))
    def _():
        stage_rows(h_a, rs_a)
        project(h_b, rs_b)


def _inproj(x, g1, w_in16, gain):
    assert _IN_SUBS <= _IN_NJ
    x_index = lambda i, j: (jnp.minimum(i, _IN_NI - 1) * _IN_SUBS + jnp.minimum(j, _IN_SUBS - 1), 0)
    q_blocks = ATTN_WIDTH // INPROJ_BN
    p_blocks = POOL_WIDTH // INPROJ_BN
    assert q_blocks * INPROJ_BN == ATTN_WIDTH and 2 * KV_WIDTH == INPROJ_BN
    w_col = lambda j: jnp.where(j < q_blocks, j,
                                jnp.where(j < q_blocks + p_blocks, j + 1, q_blocks))
    return pl.pallas_call(
        _inproj_kernel,
        name="inproj",
        grid=(_IN_NI + 1, _IN_NJ),
        in_specs=[
            pl.BlockSpec((_IN_SUB_ROWS, D_MODEL), x_index),
            pl.BlockSpec((1, D_MODEL), lambda i, j: (0, 0)),
            pl.BlockSpec((D_MODEL, INPROJ_BN), lambda i, j: (0, w_col(jnp.where(i == 0, 0, j)))),
            pl.BlockSpec((1, INPROJ_BN), lambda i, j: (0, jnp.where(i == 0, 0, j))),
        ],
        out_specs=pl.BlockSpec((INPROJ_BM, INPROJ_BN),
                               lambda i, j: (jnp.maximum(i - 1, 0), jnp.where(i == 0, 0, j))),
        out_shape=jax.ShapeDtypeStruct((SEQ, IN_COLS), BF16),
        scratch_shapes=[
            pltpu.VMEM((INPROJ_BM, D_MODEL), BF16), pltpu.VMEM((INPROJ_BM, 1), F32),
            pltpu.VMEM((INPROJ_BM, D_MODEL), BF16), pltpu.VMEM((INPROJ_BM, 1), F32),
        ],
        compiler_params=_params("arbitrary", "arbitrary"),
    )(x, g1, w_in16, gain)


MIX_ROWS = 2 * BLOCK
_MIX_STEPS = SEQ // MIX_ROWS
_MIX_EXT = MIX_ROWS + 2 * POOL_HALO
_WO_SLAB = D_MODEL // _MIX_STEPS
KEYS = 3 * BLOCK
_P_COLBLOCK = ATTN_WIDTH // POOL_WIDTH
_K_COLBLOCK = (ATTN_WIDTH + POOL_WIDTH) // KV_WIDTH
_V_COLBLOCK = _K_COLBLOCK + 1


def _mixers_kernel(sink_ref, q_ref, kp_ref, kc_ref, kn_ref, vp_ref, vc_ref, vn_ref,
                   pp_ref, pc_ref, pn_ref, pw_ref, sc_ref, wo_ref, o_ref, wo16_ref,
                   bias_ref, band_ref):
    n = pl.program_id(0)
    wo16_ref[...] = wo_ref[...].astype(BF16)

    @pl.when(n == 0)
    def _():
        qi = lax.broadcasted_iota(jnp.int32, (BLOCK, KEYS), 0)
        kj = lax.broadcasted_iota(jnp.int32, (BLOCK, KEYS), 1)
        dist = jnp.abs(kj - BLOCK - qi)
        in_band = dist <= WINDOW
        distf = dist.astype(F32)
        for v, valid in enumerate((in_band, in_band & (kj >= BLOCK), in_band & (kj < 2 * BLOCK))):
            for h in range(N_HEADS):
                bias_ref[v * N_HEADS + h] = jnp.where(
                    valid, (-ALIBI_SLOPES[h] * LOG2E) * distf, NEG_INF)

        t = lax.broadcasted_iota(jnp.int32, (MIX_ROWS, _MIX_EXT), 0)
        s_rel = lax.broadcasted_iota(jnp.int32, (MIX_ROWS, _MIX_EXT), 1) - POOL_HALO
        for gi, w in enumerate(POOL_WINDOWS):
            left = w // 2
            right = w - 1 - left
            band_ref[gi] = jnp.where((s_rel >= t - left) & (s_rel <= t + right), 1.0, 0.0).astype(BF16)

    is_first = n == 0
    is_last = n == _MIX_STEPS - 1
    row_head = lax.broadcasted_iota(jnp.int32, (GROUP * BLOCK, 1), 0) // BLOCK
    ones_kd = jnp.ones((KEYS, HEAD_DIM), BF16)

    p_cur = pc_ref[...]
    p_prev = jnp.where(is_first, jnp.zeros_like(pp_ref), pp_ref[...])
    p_next = jnp.where(is_last, jnp.zeros_like(pn_ref), pn_ref[...])
    p_ext = jnp.concatenate([p_prev, p_cur, p_next], axis=0)
    tg = n * MIX_ROWS + lax.broadcasted_iota(jnp.int32, (MIX_ROWS, 1), 0)

    def pool_group(gi):
        w = POOL_WINDOWS[gi]
        left = w // 2
        right = w - 1 - left
        cs = slice(gi * POOL_GROUP_W, (gi + 1) * POOL_GROUP_W)
        win_sum = jnp.dot(band_ref[gi], p_ext[:, cs], preferred_element_type=F32)
        cnt = jnp.minimum(tg + right + 1, SEQ) - jnp.maximum(tg - left, 0)
        u = win_sum * (1.0 / cnt.astype(F32)) - p_cur[:, cs].astype(F32)
        y = jnp.dot(u.astype(BF16), pw_ref[gi], preferred_element_type=F32)
        o_ref[:, ATTN_WIDTH + gi * POOL_GROUP_W:ATTN_WIDTH + (gi + 1) * POOL_GROUP_W] = (
            y * sc_ref[:, cs]).astype(BF16)

    assert N_POOL_GROUPS == N_KV_HEADS
    for kh in range(N_KV_HEADS):
        ksl = slice(kh * HEAD_DIM, (kh + 1) * HEAD_DIM)
        k_all = jnp.concatenate([kp_ref[:, ksl], kc_ref[:, ksl], kn_ref[:, ksl]], axis=0)
        v_all = jnp.concatenate([vp_ref[:, ksl], vc_ref[:, ksl], vn_ref[:, ksl]], axis=0)
        h0 = kh * GROUP
        sink = jnp.full((GROUP * BLOCK, 1), sink_ref[h0 + GROUP - 1] * LOG2E, F32)
        for g in range(GROUP - 1):
            sink = jnp.where(row_head == g, sink_ref[h0 + g] * LOG2E, sink)
        for qb in range(MIX_ROWS // BLOCK):
            qrows = slice(qb * BLOCK, (qb + 1) * BLOCK)
            variant = jnp.where(is_first, 1, 0) if qb == 0 else jnp.where(is_last, 2, 0)
            kw = k_all[qb * BLOCK:qb * BLOCK + KEYS]
            vw = v_all[qb * BLOCK:qb * BLOCK + KEYS]
            qs = jnp.concatenate(
                [q_ref[qrows, (h0 + g) * HEAD_DIM:(h0 + g + 1) * HEAD_DIM] for g in range(GROUP)],
                axis=0)
            s = lax.dot_general(qs, kw, (((1,), (1,)), ((), ())), preferred_element_type=F32)
            bias = bias_ref[pl.ds(variant * N_HEADS + h0, GROUP)].reshape(GROUP * BLOCK, KEYS)
            logits = s + bias
            m = jnp.maximum(jnp.max(logits, axis=-1, keepdims=True), sink)
            e = jnp.exp2(logits - m)
            ov = jnp.dot(e.astype(BF16), jnp.concatenate([vw, ones_kd], axis=1),
                         preferred_element_type=F32)
            o = ov[:, :HEAD_DIM] * (1.0 / (ov[:, HEAD_DIM:] + jnp.exp2(sink - m)))
            for g in range(GROUP):
                o_ref[qrows, (h0 + g) * HEAD_DIM:(h0 + g + 1) * HEAD_DIM] = (
                    o[g * BLOCK:(g + 1) * BLOCK].astype(BF16))
        pool_group(kh)


def _mixers(sink_logits, proj, pool_w16, pool_scale, w_out):
    last_qblock = SEQ // BLOCK - 1
    last_halo = SEQ // POOL_HALO - 1
    qb_per_step = MIX_ROWS // BLOCK
    halo_per_step = MIX_ROWS // POOL_HALO
    kv_prev = lambda col: pl.BlockSpec(
        (BLOCK, KV_WIDTH), lambda n: (jnp.maximum(n * qb_per_step - 1, 0), col))
    kv_cur = lambda col: pl.BlockSpec((MIX_ROWS, KV_WIDTH), lambda n: (n, col))
    kv_next = lambda col: pl.BlockSpec(
        (BLOCK, KV_WIDTH), lambda n: (jnp.minimum((n + 1) * qb_per_step, last_qblock), col))
    return pl.pallas_call(
        _mixers_kernel,
        name="mixers",
        grid=(_MIX_STEPS,),
        in_specs=[
            pl.BlockSpec(memory_space=pltpu.SMEM),
            pl.BlockSpec((MIX_ROWS, ATTN_WIDTH), lambda n: (n, 0)),
            kv_prev(_K_COLBLOCK), kv_cur(_K_COLBLOCK), kv_next(_K_COLBLOCK),
            kv_prev(_V_COLBLOCK), kv_cur(_V_COLBLOCK), kv_next(_V_COLBLOCK),
            pl.BlockSpec((POOL_HALO, POOL_WIDTH),
                         lambda n: (jnp.maximum(n * halo_per_step - 1, 0), _P_COLBLOCK)),
            pl.BlockSpec((MIX_ROWS, POOL_WIDTH), lambda n: (n, _P_COLBLOCK)),
            pl.BlockSpec((POOL_HALO, POOL_WIDTH),
                         lambda n: (jnp.minimum((n + 1) * halo_per_step, last_halo), _P_COLBLOCK)),
            pl.BlockSpec((N_POOL_GROUPS, POOL_GROUP_W, POOL_GROUP_W), lambda n: (0, 0, 0)),
            pl.BlockSpec((1, POOL_WIDTH), lambda n: (0, 0)),
            pl.BlockSpec((_WO_SLAB, D_MODEL), lambda n: (n, 0)),
        ],
        out_specs=[
            pl.BlockSpec((MIX_ROWS, D_MODEL), lambda n: (n, 0)),
            pl.BlockSpec((_WO_SLAB, D_MODEL), lambda n: (n, 0)),
        ],
        out_shape=[
            jax.ShapeDtypeStruct((SEQ, D_MODEL), BF16),
            jax.ShapeDtypeStruct((D_MODEL, D_MODEL), BF16),
        ],
        scratch_shapes=[
            pltpu.VMEM((3 * N_HEADS, BLOCK, KEYS), F32),
            pltpu.VMEM((N_POOL_GROUPS, MIX_ROWS, _MIX_EXT), BF16),
        ],
        compiler_params=_params("arbitrary"),
    )(sink_logits, proj, proj, proj, proj, proj, proj, proj, proj, proj, proj, pool_w16, pool_scale,
      w_out)


OUTPROJ_BM = 1024
OUTPROJ_BN = 1024
_OUTPROJ_NJ = D_MODEL // OUTPROJ_BN
_OUT_MROWS = 512


def _outproj_kernel(mx_ref, w_ref, x_ref, g2_ref, x1_ref, xg_ref, rs_ref, ss_ref):
    j = pl.program_id(1)

    @pl.when(j == 0)
    def _():
        ss_ref[...] = jnp.zeros_like(ss_ref)

    for r in range(OUTPROJ_BM // _OUT_MROWS):
        rows = slice(r * _OUT_MROWS, (r + 1) * _OUT_MROWS)
        for c in range(OUTPROJ_BN // MXU_N):
            cs = slice(c * MXU_N, (c + 1) * MXU_N)
            x1 = x_ref[rows, cs] + jnp.dot(mx_ref[rows, :], w_ref[:, cs],
                                           preferred_element_type=F32)
            x1_ref[rows, cs] = x1
            xg_ref[rows, cs] = (x1 * g2_ref[:, cs]).astype(BF16)
            sq = x1 * x1
            ss_ref[rows, :] += sum(sq[:, k * LANES:(k + 1) * LANES] for k in range(MXU_N // LANES))

    @pl.when(j == _OUTPROJ_NJ - 1)
    def _():
        ms = jnp.sum(ss_ref[...], axis=-1, keepdims=True) * (1.0 / D_MODEL)
        rs_ref[...] = lax.rsqrt(ms + RMS_EPS)


def _outproj(mixed, w_out16, x, g2):
    return pl.pallas_call(
        _outproj_kernel,
        name="outproj",
        grid=(SEQ // OUTPROJ_BM, _OUTPROJ_NJ),
        in_specs=[
            pl.BlockSpec((OUTPROJ_BM, D_MODEL), lambda i, j: (i, 0)),
            pl.BlockSpec((D_MODEL, OUTPROJ_BN), lambda i, j: (0, j)),
            pl.BlockSpec((OUTPROJ_BM, OUTPROJ_BN), lambda i, j: (i, j)),
            pl.BlockSpec((1, OUTPROJ_BN), lambda i, j: (0, j)),
        ],
        out_specs=[
            pl.BlockSpec((OUTPROJ_BM, OUTPROJ_BN), lambda i, j: (i, j)),
            pl.BlockSpec((OUTPROJ_BM, OUTPROJ_BN), lambda i, j: (i, j)),
            pl.BlockSpec((OUTPROJ_BM, 1), lambda i, j: (i, 0)),
        ],
        out_shape=[
            jax.ShapeDtypeStruct((SEQ, D_MODEL), F32),
            jax.ShapeDtypeStruct((SEQ, D_MODEL), BF16),
            jax.ShapeDtypeStruct((SEQ, 1), F32),
        ],
        scratch_shapes=[pltpu.VMEM((OUTPROJ_BM, LANES), F32)],
        compiler_params=_params("arbitrary", "arbitrary"),
    )(mixed, w_out16, x, g2)


UP_BM = 2048
_UP_MROWS = 1024
FFN_BF = 256
_N_FF_BLOCKS = D_FF // FFN_BF
DOWN_BM = 512
DOWN_BN = 512
_DOWN_K0 = (0, 2816, 5632, 8448)
_DOWN_KLEN = (2816, 2816, 2816, 2560)
_DOWN_KCHUNKS = len(_DOWN_K0)


def _ffn_up_kernel(xg_ref, rs_ref, wg_ref, wu_ref, wd_ref, a_ref, wd16_ref):
    wg = wg_ref[...].astype(BF16)
    wu = wu_ref[...].astype(BF16)
    for r in range(UP_BM // _UP_MROWS):
        rows = slice(r * _UP_MROWS, (r + 1) * _UP_MROWS)
        xg = xg_ref[rows, :]
        rs = rs_ref[rows, :]
        g = jnp.dot(xg, wg, preferred_element_type=F32) * rs
        u = jnp.dot(xg, wu, preferred_element_type=F32) * rs
        a_ref[rows, :] = (g * jax.nn.sigmoid(g) * u).astype(BF16)

    @pl.when(pl.program_id(0) == 0)
    def _():
        wd16_ref[...] = wd_ref[...].astype(BF16)


def _ffn_up(xg, rs2, w_gate, w_up, w_down):
    wd_index = lambda i, f: (jnp.where(i == 0, f, _N_FF_BLOCKS - 1), 0)
    return pl.pallas_call(
        _ffn_up_kernel,
        name="ffn_up",
        grid=(SEQ // UP_BM, _N_FF_BLOCKS),
        in_specs=[
            pl.BlockSpec((UP_BM, D_MODEL), lambda i, f: (i, 0), pipeline_mode=pl.Buffered(1)),
            pl.BlockSpec((UP_BM, 1), lambda i, f: (i, 0), pipeline_mode=pl.Buffered(1)),
            pl.BlockSpec((D_MODEL, FFN_BF), lambda i, f: (0, f)),
            pl.BlockSpec((D_MODEL, FFN_BF), lambda i, f: (0, f)),
            pl.BlockSpec((FFN_BF, D_MODEL), wd_index),
        ],
        out_specs=[
            pl.BlockSpec((UP_BM, FFN_BF), lambda i, f: (i, f)),
            pl.BlockSpec((FFN_BF, D_MODEL), wd_index),
        ],
        out_shape=[
            jax.ShapeDtypeStruct((SEQ, D_FF), BF16),
            jax.ShapeDtypeStruct((D_FF, D_MODEL), BF16),
        ],
        compiler_params=_params("arbitrary", "arbitrary"),
    )(xg, rs2, w_gate, w_up, w_down)


def _ffn_down_kernel(a_hbm, w_ref, x1_ref, o_ref, a_buf, sems):
    i = pl.program_id(0)
    j = pl.program_id(1)
    n_i = pl.num_programs(0)
    n_j = pl.num_programs(1)

    def chunk_copy(row_block, c):
        k0, kl = _DOWN_K0[c], _DOWN_KLEN[c]
        return pltpu.make_async_copy(
            a_hbm.at[pl.ds(row_block * DOWN_BM, DOWN_BM), pl.ds(k0, kl)],
            a_buf.at[row_block % 2, :, pl.ds(k0, kl)], sems.at[c])

    @pl.when((i == 0) & (j == 0))
    def _():
        for c in range(_DOWN_KCHUNKS):
            chunk_copy(0, c).start()

    @pl.when(j == 0)
    def _():
        for c in range(_DOWN_KCHUNKS):
            chunk_copy(i, c).wait()

    for c in range(_DOWN_KCHUNKS):
        @pl.when((j == n_j - _DOWN_KCHUNKS + c) & (i < n_i - 1))
        def _():
            chunk_copy(i + 1, c).start()

    a_blk = a_buf.at[i % 2]
    for c in range(DOWN_BN // MXU_N):
        cs = slice(c * MXU_N, (c + 1) * MXU_N)
        o_ref[:, cs] = x1_ref[:, cs] + jnp.dot(a_blk[...], w_ref[:, cs], preferred_element_type=F32)


def _ffn_down(a, w_down16, x1):
    assert _DOWN_KCHUNKS <= D_MODEL // DOWN_BN
    return pl.pallas_call(
        _ffn_down_kernel,
        name="ffn_down",
        grid=(SEQ // DOWN_BM, D_MODEL // DOWN_BN),
        in_specs=[
            pl.BlockSpec(memory_space=pl.ANY),
            pl.BlockSpec((D_FF, DOWN_BN), lambda i, j: (0, j)),
            pl.BlockSpec((DOWN_BM, DOWN_BN), lambda i, j: (i, j)),
        ],
        out_specs=pl.BlockSpec((DOWN_BM, DOWN_BN), lambda i, j: (i, j)),
        out_shape=jax.ShapeDtypeStruct((SEQ, D_MODEL), F32),
        scratch_shapes=[
            pltpu.VMEM((2, DOWN_BM, D_FF), BF16),
            pltpu.SemaphoreType.DMA((_DOWN_KCHUNKS,)),
        ],
        compiler_params=_params("arbitrary", "arbitrary"),
    )(a, w_down16, x1)


def kernel(x, norm1_g, w_in, q_norm_g, k_norm_g, sink_logits, pool_w, pool_scale, w_out, norm2_g,
           w_gate, w_up, w_down):
    b, s, d = x.shape
    assert (b, s, d) == (1, SEQ, D_MODEL)
    x2 = x.reshape(s, d)
    gain = jnp.concatenate([
        jnp.tile(q_norm_g, N_HEADS) * (Q_SCALE * LOG2E),
        jnp.ones((POOL_WIDTH,), F32),
        jnp.tile(k_norm_g, N_KV_HEADS),
        jnp.ones((KV_WIDTH,), F32),
    ]).reshape(1, IN_COLS)
    proj = _inproj(x2, norm1_g.reshape(1, D_MODEL), w_in.astype(BF16), gain)
    mixed, w_out16 = _mixers(sink_logits, proj, pool_w.astype(BF16),
                             pool_scale.reshape(1, POOL_WIDTH), w_out)
    x1, xg, rs2 = _outproj(mixed, w_out16, x2, norm2_g.reshape(1, D_MODEL))
    a, w_down16 = _ffn_up(xg, rs2, w_gate, w_up, w_down)
    out = _ffn_down(a, w_down16, x1)
    return out.reshape(b, s, d)
```

```python
import jax
import jax.numpy as jnp
from jax import lax
from jax.experimental import pallas as pl
from jax.experimental.pallas import tpu as pltpu

F32 = jnp.float32
BF16 = jnp.bfloat16

D_MODEL = 4096
SEQ = 8192
HEAD_DIM = 128
N_HEADS = 16
N_KV_HEADS = 4
GROUP = N_HEADS // N_KV_HEADS
ATTN_WIDTH = N_HEADS * HEAD_DIM
KV_WIDTH = N_KV_HEADS * HEAD_DIM
POOL_WIDTH = D_MODEL - ATTN_WIDTH
POOL_WINDOWS = (2, 4, 8, 16)
N_POOL_GROUPS = len(POOL_WINDOWS)
POOL_GROUP_W = POOL_WIDTH // N_POOL_GROUPS
IN_COLS = ATTN_WIDTH + 2 * KV_WIDTH + POOL_WIDTH
D_FF = 11008
WINDOW = 128
BLOCK = 128
RMS_EPS = 1e-6
NEG_INF = -1e30
Q_SCALE = HEAD_DIM ** -0.5
LOG2E = 1.4426950408889634
ALIBI_SLOPES = tuple(2.0 ** (-8.0 * h / N_HEADS) for h in range(1, N_HEADS + 1))

VMEM_LIMIT_BYTES = 56 * 1024 * 1024
MXU_N = 256
LANES = 128
POOL_HALO = 16


def _params(*semantics):
    return pltpu.CompilerParams(dimension_semantics=semantics, vmem_limit_bytes=VMEM_LIMIT_BYTES)


INPROJ_BM = 1024
INPROJ_BN = 1024
_IN_NI = SEQ // INPROJ_BM
_IN_NJ = IN_COLS // INPROJ_BN
_IN_SUBS = 4
_IN_SUB_ROWS = INPROJ_BM // _IN_SUBS
_IN_MROWS = 512
_IN_STAGE_COLS = 512
_Q_GROUPS = ATTN_WIDTH // HEAD_DIM
_K_GROUP0 = (ATTN_WIDTH + POOL_WIDTH) // HEAD_DIM
_K_GROUPS = N_KV_HEADS


def _inproj_kernel(x_ref, g1_ref, w_ref, gain_ref, o_ref, h_a, rs_a, h_b, rs_b):
    i = pl.program_id(0)
    j = pl.program_id(1)
    row0 = pl.multiple_of(jnp.minimum(j, _IN_SUBS - 1) * _IN_SUB_ROWS, _IN_SUB_ROWS)

    def stage_rows(h_ref, rs_ref):
        sq_lanes = jnp.zeros((_IN_SUB_ROWS, LANES), F32)
        for k in range(D_MODEL // _IN_STAGE_COLS):
            cs = slice(k * _IN_STAGE_COLS, (k + 1) * _IN_STAGE_COLS)
            x = x_ref[:, cs]
            sq = x * x
            sq_lanes = sq_lanes + sum(
                sq[:, t * LANES:(t + 1) * LANES] for t in range(_IN_STAGE_COLS // LANES))
            h_ref[pl.ds(row0, _IN_SUB_ROWS), cs] = (x * g1_ref[:, cs]).astype(BF16)
        ms = jnp.sum(sq_lanes, axis=-1, keepdims=True) * (1.0 / D_MODEL)
        rs_ref[pl.ds(row0, _IN_SUB_ROWS), :] = lax.rsqrt(ms + RMS_EPS)

    def project(h_ref, rs_ref):
        for c in range(INPROJ_BN // MXU_N):
            for r in range(INPROJ_BM // _IN_MROWS):
                rows = slice(r * _IN_MROWS, (r + 1) * _IN_MROWS)
                acc = jnp.dot(h_ref[rows, :], w_ref[:, c * MXU_N:(c + 1) * MXU_N],
                              preferred_element_type=F32) * rs_ref[rows, :]
                for hh in range(MXU_N // HEAD_DIM):
                    col = c * MXU_N + hh * HEAD_DIM
                    head = j * (INPROJ_BN // HEAD_DIM) + col // HEAD_DIM
                    blk = acc[:, hh * HEAD_DIM:(hh + 1) * HEAD_DIM]
                    ms = jnp.mean(blk * blk, axis=-1, keepdims=True)
                    normed = (head < _Q_GROUPS) | ((head >= _K_GROUP0) & (head < _K_GROUP0 + _K_GROUPS))
                    scale = jnp.where(normed, lax.rsqrt(ms + RMS_EPS), 1.0)
                    o_ref[rows, col:col + HEAD_DIM] = (
                        blk * scale * gain_ref[:, col:col + HEAD_DIM]).astype(BF16)

    @pl.when(i == 0)
    def _():
        stage_rows(h_a, rs_a)

    @pl.when((i > 0) & (i % 2 == 1))
    def _():
        stage_rows(h_b, rs_b)
        project(h_a, rs_a)

    @pl.when((i > 0) & (i % 2 == 0))
    def _():
        stage_rows(h_a, rs_a)
        project(h_b, rs_b)


def _inproj(x, g1, w_in16, gain):
    assert _IN_SUBS <= _IN_NJ
    x_index = lambda i, j: (jnp.minimum(i, _IN_NI - 1) * _IN_SUBS + jnp.minimum(j, _IN_SUBS - 1), 0)
    q_blocks = ATTN_WIDTH // INPROJ_BN
    p_blocks = POOL_WIDTH // INPROJ_BN
    assert q_blocks * INPROJ_BN == ATTN_WIDTH and 2 * KV_WIDTH == INPROJ_BN
    w_col = lambda j: jnp.where(j < q_blocks, j,
                                jnp.where(j < q_blocks + p_blocks, j + 1, q_blocks))
    return pl.pallas_call(
        _inproj_kernel,
        name="inproj",
        grid=(_IN_NI + 1, _IN_NJ),
        in_specs=[
            pl.BlockSpec((_IN_SUB_ROWS, D_MODEL), x_index),
            pl.BlockSpec((1, D_MODEL), lambda i, j: (0, 0)),
            pl.BlockSpec((D_MODEL, INPROJ_BN), lambda i, j: (0, w_col(jnp.where(i == 0, 0, j)))),
            pl.BlockSpec((1, INPROJ_BN), lambda i, j: (0, jnp.where(i == 0, 0, j))),
        ],
        out_specs=pl.BlockSpec((INPROJ_BM, INPROJ_BN),
                               lambda i, j: (jnp.maximum(i - 1, 0), jnp.where(i == 0, 0, j))),
        out_shape=jax.ShapeDtypeStruct((SEQ, IN_COLS), BF16),
        scratch_shapes=[
            pltpu.VMEM((INPROJ_BM, D_MODEL), BF16), pltpu.VMEM((INPROJ_BM, 1), F32),
            pltpu.VMEM((INPROJ_BM, D_MODEL), BF16), pltpu.VMEM((INPROJ_BM, 1), F32),
        ],
        compiler_params=_params("arbitrary", "arbitrary"),
    )(x, g1, w_in16, gain)


MIX_ROWS = 2 * BLOCK
_MIX_STEPS = SEQ // MIX_ROWS
_MIX_EXT = MIX_ROWS + 2 * POOL_HALO
_WO_SLAB = D_MODEL // _MIX_STEPS
KEYS = 3 * BLOCK
_P_COLBLOCK = ATTN_WIDTH // POOL_WIDTH
_K_COLBLOCK = (ATTN_WIDTH + POOL_WIDTH) // KV_WIDTH
_V_COLBLOCK = _K_COLBLOCK + 1


def _mixers_kernel(sink_ref, q_ref, kp_ref, kc_ref, kn_ref, vp_ref, vc_ref, vn_ref,
                   pp_ref, pc_ref, pn_ref, pw_ref, sc_ref, wo_ref, o_ref, wo16_ref,
                   bias_ref, band_ref):
    n = pl.program_id(0)
    wo16_ref[...] = wo_ref[...].astype(BF16)

    @pl.when(n == 0)
    def _():
        qi = lax.broadcasted_iota(jnp.int32, (BLOCK, KEYS), 0)
        kj = lax.broadcasted_iota(jnp.int32, (BLOCK, KEYS), 1)
        dist = jnp.abs(kj - BLOCK - qi)
        in_band = dist <= WINDOW
        distf = dist.astype(F32)
        for v, valid in enumerate((in_band, in_band & (kj >= BLOCK), in_band & (kj < 2 * BLOCK))):
            for h in range(N_HEADS):
                bias_ref[v * N_HEADS + h] = jnp.where(
                    valid, (-ALIBI_SLOPES[h] * LOG2E) * distf, NEG_INF)

        t = lax.broadcasted_iota(jnp.int32, (MIX_ROWS, _MIX_EXT), 0)
        s_rel = lax.broadcasted_iota(jnp.int32, (MIX_ROWS, _MIX_EXT), 1) - POOL_HALO
        for gi, w in enumerate(POOL_WINDOWS):
            left = w // 2
            right = w - 1 - left
            band_ref[gi] = jnp.where((s_rel >= t - left) & (s_rel <= t + right), 1.0, 0.0).astype(BF16)

    is_first = n == 0
    is_last = n == _MIX_STEPS - 1
    row_head = lax.broadcasted_iota(jnp.int32, (GROUP * BLOCK, 1), 0) // BLOCK
    ones_kd = jnp.ones((KEYS, HEAD_DIM), BF16)

    p_cur = pc_ref[...]
    p_prev = jnp.where(is_first, jnp.zeros_like(pp_ref), pp_ref[...])
    p_next = jnp.where(is_last, jnp.zeros_like(pn_ref), pn_ref[...])
    p_ext = jnp.concatenate([p_prev, p_cur, p_next], axis=0)
    tg = n * MIX_ROWS + lax.broadcasted_iota(jnp.int32, (MIX_ROWS, 1), 0)

    def pool_group(gi):
        w = POOL_WINDOWS[gi]
        left = w // 2
        right = w - 1 - left
        cs = slice(gi * POOL_GROUP_W, (gi + 1) * POOL_GROUP_W)
        win_sum = jnp.dot(band_ref[gi], p_ext[:, cs], preferred_element_type=F32)
        cnt = jnp.minimum(tg + right + 1, SEQ) - jnp.maximum(tg - left, 0)
        u = win_sum * (1.0 / cnt.astype(F32)) - p_cur[:, cs].astype(F32)
        y = jnp.dot(u.astype(BF16), pw_ref[gi], preferred_element_type=F32)
        o_ref[:, ATTN_WIDTH + gi * POOL_GROUP_W:ATTN_WIDTH + (gi + 1) * POOL_GROUP_W] = (
            y * sc_ref[:, cs]).astype(BF16)

    assert N_POOL_GROUPS == N_KV_HEADS
    for kh in range(N_KV_HEADS):
        ksl = slice(kh * HEAD_DIM, (kh + 1) * HEAD_DIM)
        k_all = jnp.concatenate([kp_ref[:, ksl], kc_ref[:, ksl], kn_ref[:, ksl]], axis=0)
        v_all = jnp.concatenate([vp_ref[:, ksl], vc_ref[:, ksl], vn_ref[:, ksl]], axis=0)
        h0 = kh * GROUP
        sink = jnp.full((GROUP * BLOCK, 1), sink_ref[h0 + GROUP - 1] * LOG2E, F32)
        for g in range(GROUP - 1):
            sink = jnp.where(row_head == g, sink_ref[h0 + g] * LOG2E, sink)
        for qb in range(MIX_ROWS // BLOCK):
            qrows = slice(qb * BLOCK, (qb + 1) * BLOCK)
            variant = jnp.where(is_first, 1, 0) if qb == 0 else jnp.where(is_last, 2, 0)
            kw = k_all[qb * BLOCK:qb * BLOCK + KEYS]
            vw = v_all[qb * BLOCK:qb * BLOCK + KEYS]
            qs = jnp.concatenate(
                [q_ref[qrows, (h0 + g) * HEAD_DIM:(h0 + g + 1) * HEAD_DIM] for g in range(GROUP)],
                axis=0)
            s = lax.dot_general(qs, kw, (((1,), (1,)), ((), ())), preferred_element_type=F32)
            bias = bias_ref[pl.ds(variant * N_HEADS + h0, GROUP)].reshape(GROUP * BLOCK, KEYS)
            logits = s + bias
            m = jnp.maximum(jnp.max(logits, axis=-1, keepdims=True), sink)
            e = jnp.exp2(logits - m)
            ov = jnp.dot(e.astype(BF16), jnp.concatenate([vw, ones_kd], axis=1),
                         preferred_element_type=F32)
            o = ov[:, :HEAD_DIM] * (1.0 / (ov[:, HEAD_DIM:] + jnp.exp2(sink - m)))
            for g in range(GROUP):
                o_ref[qrows, (h0 + g) * HEAD_DIM:(h0 + g + 1) * HEAD_DIM] = (
                    o[g * BLOCK:(g + 1) * BLOCK].astype(BF16))
        pool_group(kh)


def _mixers(sink_logits, proj, pool_w16, pool_scale, w_out):
    last_qblock = SEQ // BLOCK - 1
    last_halo = SEQ // POOL_HALO - 1
    qb_per_step = MIX_ROWS // BLOCK
    halo_per_step = MIX_ROWS // POOL_HALO
    kv_prev = lambda col: pl.BlockSpec(
        (BLOCK, KV_WIDTH), lambda n: (jnp.maximum(n * qb_per_step - 1, 0), col))
    kv_cur = lambda col: pl.BlockSpec((MIX_ROWS, KV_WIDTH), lambda n: (n, col))
    kv_next = lambda col: pl.BlockSpec(
        (BLOCK, KV_WIDTH), lambda n: (jnp.minimum((n + 1) * qb_per_step, last_qblock), col))
    return pl.pallas_call(
        _mixers_kernel,
        name="mixers",
        grid=(_MIX_STEPS,),
        in_specs=[
            pl.BlockSpec(memory_space=pltpu.SMEM),
            pl.BlockSpec((MIX_ROWS, ATTN_WIDTH), lambda n: (n, 0)),
            kv_prev(_K_COLBLOCK), kv_cur(_K_COLBLOCK), kv_next(_K_COLBLOCK),
            kv_prev(_V_COLBLOCK), kv_cur(_V_COLBLOCK), kv_next(_V_COLBLOCK),
            pl.BlockSpec((POOL_HALO, POOL_WIDTH),
                         lambda n: (jnp.maximum(n * halo_per_step - 1, 0), _P_COLBLOCK)),
            pl.BlockSpec((MIX_ROWS, POOL_WIDTH), lambda n: (n, _P_COLBLOCK)),
            pl.BlockSpec((POOL_HALO, POOL_WIDTH),
                         lambda n: (jnp.minimum((n + 1) * halo_per_step, last_halo), _P_COLBLOCK)),
            pl.BlockSpec((N_POOL_GROUPS, POOL_GROUP_W, POOL_GROUP_W), lambda n: (0, 0, 0)),
            pl.BlockSpec((1, POOL_WIDTH), lambda n: (0, 0)),
            pl.BlockSpec((_WO_SLAB, D_MODEL), lambda n: (n, 0)),
        ],
        out_specs=[
            pl.BlockSpec((MIX_ROWS, D_MODEL), lambda n: (n, 0)),
            pl.BlockSpec((_WO_SLAB, D_MODEL), lambda n: (n, 0)),
        ],
        out_shape=[
            jax.ShapeDtypeStruct((SEQ, D_MODEL), BF16),
            jax.ShapeDtypeStruct((D_MODEL, D_MODEL), BF16),
        ],
        scratch_shapes=[
            pltpu.VMEM((3 * N_HEADS, BLOCK, KEYS), F32),
            pltpu.VMEM((N_POOL_GROUPS, MIX_ROWS, _MIX_EXT), BF16),
        ],
        compiler_params=_params("arbitrary"),
    )(sink_logits, proj, proj, proj, proj, proj, proj, proj, proj, proj, proj, pool_w16, pool_scale,
      w_out)


OUTPROJ_BM = 1024
OUTPROJ_BN = 1024
_OUTPROJ_NJ = D_MODEL // OUTPROJ_BN
_OUT_MROWS = 1024


def _outproj_kernel(mx_ref, w_ref, x_ref, g2_ref, x1_ref, xg_ref, rs_ref, ss_ref):
    j = pl.program_id(1)

    @pl.when(j == 0)
    def _():
        ss_ref[...] = jnp.zeros_like(ss_ref)

    for r in range(OUTPROJ_BM // _OUT_MROWS):
        rows = slice(r * _OUT_MROWS, (r + 1) * _OUT_MROWS)
        for c in range(OUTPROJ_BN // MXU_N):
            cs = slice(c * MXU_N, (c + 1) * MXU_N)
            x1 = x_ref[rows, cs] + jnp.dot(mx_ref[rows, :], w_ref[:, cs],
                                           preferred_element_type=F32)
            x1_ref[rows, cs] = x1
            xg_ref[rows, cs] = (x1 * g2_ref[:, cs]).astype(BF16)
            sq = x1 * x1
            ss_ref[rows, :] += sum(sq[:, k * LANES:(k + 1) * LANES] for k in range(MXU_N // LANES))

    @pl.when(j == _OUTPROJ_NJ - 1)
    def _():
        ms = jnp.sum(ss_ref[...], axis=-1, keepdims=True) * (1.0 / D_MODEL)
        rs_ref[...] = lax.rsqrt(ms + RMS_EPS)


def _outproj(mixed, w_out16, x, g2):
    return pl.pallas_call(
        _outproj_kernel,
        name="outproj",
        grid=(SEQ // OUTPROJ_BM, _OUTPROJ_NJ),
        in_specs=[
            pl.BlockSpec((OUTPROJ_BM, D_MODEL), lambda i, j: (i, 0)),
            pl.BlockSpec((D_MODEL, OUTPROJ_BN), lambda i, j: (0, j)),
            pl.BlockSpec((OUTPROJ_BM, OUTPROJ_BN), lambda i, j: (i, j)),
            pl.BlockSpec((1, OUTPROJ_BN), lambda i, j: (0, j)),
        ],
        out_specs=[
            pl.BlockSpec((OUTPROJ_BM, OUTPROJ_BN), lambda i, j: (i, j)),
            pl.BlockSpec((OUTPROJ_BM, OUTPROJ_BN), lambda i, j: (i, j)),
            pl.BlockSpec((OUTPROJ_BM, 1), lambda i, j: (i, 0)),
        ],
        out_shape=[
            jax.ShapeDtypeStruct((SEQ, D_MODEL), F32),
            jax.ShapeDtypeStruct((SEQ, D_MODEL), BF16),
            jax.ShapeDtypeStruct((SEQ, 1), F32),
        ],
        scratch_shapes=[pltpu.VMEM((OUTPROJ_BM, LANES), F32)],
        compiler_params=_params("arbitrary", "arbitrary"),
    )(mixed, w_out16, x, g2)


UP_BM = 2048
_UP_MROWS = 1024
FFN_BF = 256
_N_FF_BLOCKS = D_FF // FFN_BF
DOWN_BM = 512
DOWN_BN = 512
_DOWN_KCHUNKS = 4
_DOWN_KCHUNK = -(-D_FF // (_DOWN_KCHUNKS * MXU_N)) * MXU_N
_DOWN_K0 = tuple(range(0, D_FF, _DOWN_KCHUNK))
_DOWN_KLEN = tuple(min(_DOWN_KCHUNK, D_FF - k0) for k0 in _DOWN_K0)
assert len(_DOWN_K0) == _DOWN_KCHUNKS


def _ffn_up_kernel(xg_ref, rs_ref, wg_ref, wu_ref, wd_ref, a_ref, wd16_ref):
    wg = wg_ref[...].astype(BF16)
    wu = wu_ref[...].astype(BF16)
    for r in range(UP_BM // _UP_MROWS):
        rows = slice(r * _UP_MROWS, (r + 1) * _UP_MROWS)
        xg = xg_ref[rows, :]
        rs = rs_ref[rows, :]
        g = jnp.dot(xg, wg, preferred_element_type=F32) * rs
        u = jnp.dot(xg, wu, preferred_element_type=F32) * rs
        a_ref[rows, :] = (g * jax.nn.sigmoid(g) * u).astype(BF16)

    @pl.when(pl.program_id(0) == 0)
    def _():
        wd16_ref[...] = wd_ref[...].astype(BF16)


def _ffn_up(xg, rs2, w_gate, w_up, w_down):
    wd_index = lambda i, f: (jnp.where(i == 0, f, _N_FF_BLOCKS - 1), 0)
    return pl.pallas_call(
        _ffn_up_kernel,
        name="ffn_up",
        grid=(SEQ // UP_BM, _N_FF_BLOCKS),
        in_specs=[
            pl.BlockSpec((UP_BM, D_MODEL), lambda i, f: (i, 0), pipeline_mode=pl.Buffered(1)),
            pl.BlockSpec((UP_BM, 1), lambda i, f: (i, 0), pipeline_mode=pl.Buffered(1)),
            pl.BlockSpec((D_MODEL, FFN_BF), lambda i, f: (0, f)),
            pl.BlockSpec((D_MODEL, FFN_BF), lambda i, f: (0, f)),
            pl.BlockSpec((FFN_BF, D_MODEL), wd_index),
        ],
        out_specs=[
            pl.BlockSpec((UP_BM, FFN_BF), lambda i, f: (i, f)),
            pl.BlockSpec((FFN_BF, D_MODEL), wd_index),
        ],
        out_shape=[
            jax.ShapeDtypeStruct((SEQ, D_FF), BF16),
            jax.ShapeDtypeStruct((D_FF, D_MODEL), BF16),
        ],
        compiler_params=_params("arbitrary", "arbitrary"),
    )(xg, rs2, w_gate, w_up, w_down)


def _ffn_down_kernel(a_hbm, w_ref, x1_ref, o_ref, a_buf, sems):
    i = pl.program_id(0)
    j = pl.program_id(1)
    n_i = pl.num_programs(0)
    n_j = pl.num_programs(1)

    def chunk_copy(row_block, c):
        k0, kl = _DOWN_K0[c], _DOWN_KLEN[c]
        return pltpu.make_async_copy(
            a_hbm.at[pl.ds(row_block * DOWN_BM, DOWN_BM), pl.ds(k0, kl)],
            a_buf.at[row_block % 2, :, pl.ds(k0, kl)], sems.at[c])

    @pl.when((i == 0) & (j == 0))
    def _():
        for c in range(_DOWN_KCHUNKS):
            chunk_copy(0, c).start()

    @pl.when(j == 0)
    def _():
        for c in range(_DOWN_KCHUNKS):
            chunk_copy(i, c).wait()

    for c in range(_DOWN_KCHUNKS):
        @pl.when((j == n_j - _DOWN_KCHUNKS + c) & (i < n_i - 1))
        def _():
            chunk_copy(i + 1, c).start()

    a_blk = a_buf.at[i % 2]
    for c in range(DOWN_BN // MXU_N):
        cs = slice(c * MXU_N, (c + 1) * MXU_N)
        o_ref[:, cs] = x1_ref[:, cs] + jnp.dot(a_blk[...], w_ref[:, cs], preferred_element_type=F32)


def _ffn_down(a, w_down16, x1):
    assert _DOWN_KCHUNKS <= D_MODEL // DOWN_BN
    return pl.pallas_call(
        _ffn_down_kernel,
        name="ffn_down",
        grid=(SEQ // DOWN_BM, D_MODEL // DOWN_BN),
        in_specs=[
            pl.BlockSpec(memory_space=pl.ANY),
            pl.BlockSpec((D_FF, DOWN_BN), lambda i, j: (0, j)),
            pl.BlockSpec((DOWN_BM, DOWN_BN), lambda i, j: (i, j)),
        ],
        out_specs=pl.BlockSpec((DOWN_BM, DOWN_BN), lambda i, j: (i, j)),
        out_shape=jax.ShapeDtypeStruct((SEQ, D_MODEL), F32),
        scratch_shapes=[
            pltpu.VMEM((2, DOWN_BM, D_FF), BF16),
            pltpu.SemaphoreType.DMA((_DOWN_KCHUNKS,)),
        ],
        compiler_params=_params("arbitrary", "arbitrary"),
    )(a, w_down16, x1)


def kernel(x, norm1_g, w_in, q_norm_g, k_norm_g, sink_logits, pool_w, pool_scale, w_out, norm2_g,
           w_gate, w_up, w_down):
    b, s, d = x.shape
    assert (b, s, d) == (1, SEQ, D_MODEL)
    x2 = x.reshape(s, d)
    gain = jnp.concatenate([
        jnp.tile(q_norm_g, N_HEADS) * (Q_SCALE * LOG2E),
        jnp.ones((POOL_WIDTH,), F32),
        jnp.tile(k_norm_g, N_KV_HEADS),
        jnp.ones((KV_WIDTH,), F32),
    ]).reshape(1, IN_COLS)
    proj = _inproj(x2, norm1_g.reshape(1, D_MODEL), w_in.astype(BF16), gain)
    mixed, w_out16 = _mixers(sink_logits, proj, pool_w.astype(BF16),
                             pool_scale.reshape(1, POOL_WIDTH), w_out)
    x1, xg, rs2 = _outproj(mixed, w_out16, x2, norm2_g.reshape(1, D_MODEL))
    a, w_down16 = _ffn_up(xg, rs2, w_gate, w_up, w_down)
    out = _ffn_down(a, w_down16, x1)
    return out.reshape(b, s, d)
```

```python
import jax
import jax.numpy as jnp
from jax import lax
from jax.experimental import pallas as pl
from jax.experimental.pallas import tpu as pltpu

F32 = jnp.float32
BF16 = jnp.bfloat16

D_MODEL = 4096
SEQ = 8192
HEAD_DIM = 128
N_HEADS = 16
N_KV_HEADS = 4
GROUP = N_HEADS // N_KV_HEADS
ATTN_WIDTH = N_HEADS * HEAD_DIM
KV_WIDTH = N_KV_HEADS * HEAD_DIM
POOL_WIDTH = D_MODEL - ATTN_WIDTH
POOL_WINDOWS = (2, 4, 8, 16)
N_POOL_GROUPS = len(POOL_WINDOWS)
POOL_GROUP_W = POOL_WIDTH // N_POOL_GROUPS
IN_COLS = ATTN_WIDTH + 2 * KV_WIDTH + POOL_WIDTH
D_FF = 11008
WINDOW = 128
BLOCK = 128
RMS_EPS = 1e-6
NEG_INF = -1e30
Q_SCALE = HEAD_DIM ** -0.5
LOG2E = 1.4426950408889634
ALIBI_SLOPES = tuple(2.0 ** (-8.0 * h / N_HEADS) for h in range(1, N_HEADS + 1))

VMEM_LIMIT_BYTES = 56 * 1024 * 1024
MXU_N = 256
LANES = 128
POOL_HALO = 16


def _params(*semantics):
    return pltpu.CompilerParams(dimension_semantics=semantics, vmem_limit_bytes=VMEM_LIMIT_BYTES)


INPROJ_BM = 1024
INPROJ_BN = 1024
_IN_NI = SEQ // INPROJ_BM
_IN_NJ = IN_COLS // INPROJ_BN
_IN_SUBS = 4
_IN_SUB_ROWS = INPROJ_BM // _IN_SUBS
_IN_MROWS = 512
_IN_STAGE_COLS = 512
_Q_GROUPS = ATTN_WIDTH // HEAD_DIM
_K_GROUP0 = (ATTN_WIDTH + POOL_WIDTH) // HEAD_DIM
_K_GROUPS = N_KV_HEADS


def _w_col(j, bn):
    q_blocks, p_blocks, kv_blocks = ATTN_WIDTH // bn, POOL_WIDTH // bn, 2 * KV_WIDTH // bn
    assert q_blocks * bn == ATTN_WIDTH and p_blocks * bn == POOL_WIDTH and kv_blocks * bn == 2 * KV_WIDTH
    return jnp.where(j < q_blocks, j, jnp.where(j < q_blocks + p_blocks, j + kv_blocks, j - p_blocks))


def _stage_rows(x_ref, g1_ref, h_ref, rs_ref, row0, n_rows):
    sq_lanes = jnp.zeros((n_rows, LANES), F32)
    for k in range(D_MODEL // _IN_STAGE_COLS):
        cs = slice(k * _IN_STAGE_COLS, (k + 1) * _IN_STAGE_COLS)
        x = x_ref[:, cs]
        sq = x * x
        sq_lanes = sq_lanes + sum(
            sq[:, t * LANES:(t + 1) * LANES] for t in range(_IN_STAGE_COLS // LANES))
        h_ref[pl.ds(row0, n_rows), cs] = (x * g1_ref[:, cs]).astype(BF16)
    ms = jnp.sum(sq_lanes, axis=-1, keepdims=True) * (1.0 / D_MODEL)
    rs_ref[pl.ds(row0, n_rows), :] = lax.rsqrt(ms + RMS_EPS)


def _store_groups(acc, first_group, gain_ref, o_ref, rows, col0):
    for hh in range(MXU_N // HEAD_DIM):
        col = col0 + hh * HEAD_DIM
        group = first_group + hh
        blk = acc[:, hh * HEAD_DIM:(hh + 1) * HEAD_DIM]
        ms = jnp.mean(blk * blk, axis=-1, keepdims=True)
        normed = (group < _Q_GROUPS) | ((group >= _K_GROUP0) & (group < _K_GROUP0 + _K_GROUPS))
        scale = jnp.where(normed, lax.rsqrt(ms + RMS_EPS), 1.0)
        o_ref[rows, col:col + HEAD_DIM] = (
            blk * scale * gain_ref[:, col:col + HEAD_DIM]).astype(BF16)


def _inproj_head_kernel(x_ref, g1_ref, w_ref, gain_ref, o_ref, w16_ref, h_ref, rs_ref):
    j = pl.program_id(0)

    @pl.when(j == 0)
    def _():
        _stage_rows(x_ref, g1_ref, h_ref, rs_ref, 0, INPROJ_BM)

    w16 = w_ref[...].astype(BF16)
    w16_ref[...] = w16
    for r in range(INPROJ_BM // _IN_MROWS):
        rows = slice(r * _IN_MROWS, (r + 1) * _IN_MROWS)
        acc = jnp.dot(h_ref[rows, :], w16, preferred_element_type=F32) * rs_ref[rows, :]
        _store_groups(acc, j * (MXU_N // HEAD_DIM), gain_ref, o_ref, rows, 0)


def _inproj_head(x, g1, w_in, gain):
    w_index = lambda j: (0, _w_col(j, MXU_N))
    return pl.pallas_call(
        _inproj_head_kernel,
        name="inproj_head",
        grid=(IN_COLS // MXU_N,),
        in_specs=[
            pl.BlockSpec((INPROJ_BM, D_MODEL), lambda j: (0, 0), pipeline_mode=pl.Buffered(1)),
            pl.BlockSpec((1, D_MODEL), lambda j: (0, 0)),
            pl.BlockSpec((D_MODEL, MXU_N), w_index),
            pl.BlockSpec((1, MXU_N), lambda j: (0, j)),
        ],
        out_specs=[
            pl.BlockSpec((INPROJ_BM, MXU_N), lambda j: (0, j)),
            pl.BlockSpec((D_MODEL, MXU_N), w_index),
        ],
        out_shape=[
            jax.ShapeDtypeStruct((INPROJ_BM, IN_COLS), BF16),
            jax.ShapeDtypeStruct((D_MODEL, IN_COLS), BF16),
        ],
        scratch_shapes=[pltpu.VMEM((INPROJ_BM, D_MODEL), BF16), pltpu.VMEM((INPROJ_BM, 1), F32)],
        compiler_params=_params("arbitrary"),
    )(x, g1, w_in, gain)


def _inproj_kernel(x_ref, g1_ref, w_ref, gain_ref, head_ref, o_ref, h_a, rs_a, h_b, rs_b):
    i = pl.program_id(0)
    j = pl.program_id(1)
    row0 = pl.multiple_of(jnp.minimum(j, _IN_SUBS - 1) * _IN_SUB_ROWS, _IN_SUB_ROWS)

    def stage_rows(h_ref, rs_ref):
        _stage_rows(x_ref, g1_ref, h_ref, rs_ref, row0, _IN_SUB_ROWS)

    def project(h_ref, rs_ref):
        for c in range(INPROJ_BN // MXU_N):
            for r in range(INPROJ_BM // _IN_MROWS):
                rows = slice(r * _IN_MROWS, (r + 1) * _IN_MROWS)
                acc = jnp.dot(h_ref[rows, :], w_ref[:, c * MXU_N:(c + 1) * MXU_N],
                              preferred_element_type=F32) * rs_ref[rows, :]
                first_group = j * (INPROJ_BN // HEAD_DIM) + c * (MXU_N // HEAD_DIM)
                _store_groups(acc, first_group, gain_ref, o_ref, rows, c * MXU_N)

    @pl.when(i == 0)
    def _():
        stage_rows(h_a, rs_a)
        o_ref[...] = head_ref[...]

    @pl.when((i > 0) & (i % 2 == 1))
    def _():
        stage_rows(h_b, rs_b)
        project(h_a, rs_a)

    @pl.when((i > 0) & (i % 2 == 0))
    def _():
        stage_rows(h_a, rs_a)
        project(h_b, rs_b)


def _inproj(x, g1, w_in16, gain, proj_head):
    assert _IN_SUBS <= _IN_NJ
    x_index = lambda i, j: (
        jnp.minimum(i + 1, _IN_NI - 1) * _IN_SUBS + jnp.minimum(j, _IN_SUBS - 1), 0)
    pinned = lambda i, j: jnp.where(i == 0, 0, j)
    return pl.pallas_call(
        _inproj_kernel,
        name="inproj",
        grid=(_IN_NI, _IN_NJ),
        in_specs=[
            pl.BlockSpec((_IN_SUB_ROWS, D_MODEL), x_index),
            pl.BlockSpec((1, D_MODEL), lambda i, j: (0, 0)),
            pl.BlockSpec((D_MODEL, INPROJ_BN), lambda i, j: (0, _w_col(pinned(i, j), INPROJ_BN))),
            pl.BlockSpec((1, INPROJ_BN), lambda i, j: (0, pinned(i, j))),
            pl.BlockSpec((INPROJ_BM, INPROJ_BN), lambda i, j: (0, jnp.where(i == 0, j, _IN_NJ - 1))),
        ],
        out_specs=pl.BlockSpec((INPROJ_BM, INPROJ_BN), lambda i, j: (i, j)),
        out_shape=jax.ShapeDtypeStruct((SEQ, IN_COLS), BF16),
        scratch_shapes=[
            pltpu.VMEM((INPROJ_BM, D_MODEL), BF16), pltpu.VMEM((INPROJ_BM, 1), F32),
            pltpu.VMEM((INPROJ_BM, D_MODEL), BF16), pltpu.VMEM((INPROJ_BM, 1), F32),
        ],
        compiler_params=_params("arbitrary", "arbitrary"),
    )(x, g1, w_in16, gain, proj_head)


MIX_ROWS = 2 * BLOCK
_MIX_STEPS = SEQ // MIX_ROWS
_MIX_EXT = MIX_ROWS + 2 * POOL_HALO
_WO_SLAB = D_MODEL // _MIX_STEPS
KEYS = 3 * BLOCK
_P_COLBLOCK = ATTN_WIDTH // POOL_WIDTH
_K_COLBLOCK = (ATTN_WIDTH + POOL_WIDTH) // KV_WIDTH
_V_COLBLOCK = _K_COLBLOCK + 1


def _mixers_kernel(sink_ref, q_ref, kp_ref, kc_ref, kn_ref, vp_ref, vc_ref, vn_ref,
                   pp_ref, pc_ref, pn_ref, pw_ref, sc_ref, wo_ref, o_ref, wo16_ref,
                   bias_ref, band_ref):
    n = pl.program_id(0)
    wo16_ref[...] = wo_ref[...].astype(BF16)

    @pl.when(n == 0)
    def _():
        qi = lax.broadcasted_iota(jnp.int32, (BLOCK, KEYS), 0)
        kj = lax.broadcasted_iota(jnp.int32, (BLOCK, KEYS), 1)
        dist = jnp.abs(kj - BLOCK - qi)
        in_band = dist <= WINDOW
        distf = dist.astype(F32)
        for v, valid in enumerate((in_band, in_band & (kj >= BLOCK), in_band & (kj < 2 * BLOCK))):
            for h in range(N_HEADS):
                bias_ref[v * N_HEADS + h] = jnp.where(
                    valid, (-ALIBI_SLOPES[h] * LOG2E) * distf, NEG_INF)

        t = lax.broadcasted_iota(jnp.int32, (MIX_ROWS, _MIX_EXT), 0)
        s_rel = lax.broadcasted_iota(jnp.int32, (MIX_ROWS, _MIX_EXT), 1) - POOL_HALO
        for gi, w in enumerate(POOL_WINDOWS):
            left = w // 2
            right = w - 1 - left
            band_ref[gi] = jnp.where((s_rel >= t - left) & (s_rel <= t + right), 1.0, 0.0).astype(BF16)

    is_first = n == 0
    is_last = n == _MIX_STEPS - 1
    row_head = lax.broadcasted_iota(jnp.int32, (GROUP * BLOCK, 1), 0) // BLOCK
    ones_kd = jnp.ones((KEYS, HEAD_DIM), BF16)

    p_cur = pc_ref[...]
    p_prev = jnp.where(is_first, jnp.zeros_like(pp_ref), pp_ref[...])
    p_next = jnp.where(is_last, jnp.zeros_like(pn_ref), pn_ref[...])
    p_ext = jnp.concatenate([p_prev, p_cur, p_next], axis=0)
    tg = n * MIX_ROWS + lax.broadcasted_iota(jnp.int32, (MIX_ROWS, 1), 0)

    def pool_group(gi):
        w = POOL_WINDOWS[gi]
        left = w // 2
        right = w - 1 - left
        cs = slice(gi * POOL_GROUP_W, (gi + 1) * POOL_GROUP_W)
        win_sum = jnp.dot(band_ref[gi], p_ext[:, cs], preferred_element_type=F32)
        cnt = jnp.minimum(tg + right + 1, SEQ) - jnp.maximum(tg - left, 0)
        u = win_sum * (1.0 / cnt.astype(F32)) - p_cur[:, cs].astype(F32)
        y = jnp.dot(u.astype(BF16), pw_ref[gi], preferred_element_type=F32)
        o_ref[:, ATTN_WIDTH + gi * POOL_GROUP_W:ATTN_WIDTH + (gi + 1) * POOL_GROUP_W] = (
            y * sc_ref[:, cs]).astype(BF16)

    assert N_POOL_GROUPS == N_KV_HEADS
    for kh in range(N_KV_HEADS):
        ksl = slice(kh * HEAD_DIM, (kh + 1) * HEAD_DIM)
        k_all = jnp.concatenate([kp_ref[:, ksl], kc_ref[:, ksl], kn_ref[:, ksl]], axis=0)
        v_all = jnp.concatenate([vp_ref[:, ksl], vc_ref[:, ksl], vn_ref[:, ksl]], axis=0)
        h0 = kh * GROUP
        sink = jnp.full((GROUP * BLOCK, 1), sink_ref[h0 + GROUP - 1] * LOG2E, F32)
        for g in range(GROUP - 1):
            sink = jnp.where(row_head == g, sink_ref[h0 + g] * LOG2E, sink)
        for qb in range(MIX_ROWS // BLOCK):
            qrows = slice(qb * BLOCK, (qb + 1) * BLOCK)
            variant = jnp.where(is_first, 1, 0) if qb == 0 else jnp.where(is_last, 2, 0)
            kw = k_all[qb * BLOCK:qb * BLOCK + KEYS]
            vw = v_all[qb * BLOCK:qb * BLOCK + KEYS]
            qs = jnp.concatenate(
                [q_ref[qrows, (h0 + g) * HEAD_DIM:(h0 + g + 1) * HEAD_DIM] for g in range(GROUP)],
                axis=0)
            s = lax.dot_general(qs, kw, (((1,), (1,)), ((), ())), preferred_element_type=F32)
            bias = bias_ref[pl.ds(variant * N_HEADS + h0, GROUP)].reshape(GROUP * BLOCK, KEYS)
            logits = s + bias
            m = jnp.maximum(jnp.max(logits, axis=-1, keepdims=True), sink)
            e = jnp.exp2(logits - m)
            ov = jnp.dot(e.astype(BF16), jnp.concatenate([vw, ones_kd], axis=1),
                         preferred_element_type=F32)
            o = ov[:, :HEAD_DIM] * (1.0 / (ov[:, HEAD_DIM:] + jnp.exp2(sink - m)))
            for g in range(GROUP):
                o_ref[qrows, (h0 + g) * HEAD_DIM:(h0 + g + 1) * HEAD_DIM] = (
                    o[g * BLOCK:(g + 1) * BLOCK].astype(BF16))
        pool_group(kh)


def _mixers(sink_logits, proj, pool_w16, pool_scale, w_out):
    last_qblock = SEQ // BLOCK - 1
    last_halo = SEQ // POOL_HALO - 1
    qb_per_step = MIX_ROWS // BLOCK
    halo_per_step = MIX_ROWS // POOL_HALO
    kv_prev = lambda col: pl.BlockSpec(
        (BLOCK, KV_WIDTH), lambda n: (jnp.maximum(n * qb_per_step - 1, 0), col))
    kv_cur = lambda col: pl.BlockSpec((MIX_ROWS, KV_WIDTH), lambda n: (n, col))
    kv_next = lambda col: pl.BlockSpec(
        (BLOCK, KV_WIDTH), lambda n: (jnp.minimum((n + 1) * qb_per_step, last_qblock), col))
    return pl.pallas_call(
        _mixers_kernel,
        name="mixers",
        grid=(_MIX_STEPS,),
        in_specs=[
            pl.BlockSpec(memory_space=pltpu.SMEM),
            pl.BlockSpec((MIX_ROWS, ATTN_WIDTH), lambda n: (n, 0)),
            kv_prev(_K_COLBLOCK), kv_cur(_K_COLBLOCK), kv_next(_K_COLBLOCK),
            kv_prev(_V_COLBLOCK), kv_cur(_V_COLBLOCK), kv_next(_V_COLBLOCK),
            pl.BlockSpec((POOL_HALO, POOL_WIDTH),
                         lambda n: (jnp.maximum(n * halo_per_step - 1, 0), _P_COLBLOCK)),
            pl.BlockSpec((MIX_ROWS, POOL_WIDTH), lambda n: (n, _P_COLBLOCK)),
            pl.BlockSpec((POOL_HALO, POOL_WIDTH),
                         lambda n: (jnp.minimum((n + 1) * halo_per_step, last_halo), _P_COLBLOCK)),
            pl.BlockSpec((N_POOL_GROUPS, POOL_GROUP_W, POOL_GROUP_W), lambda n: (0, 0, 0)),
            pl.BlockSpec((1, POOL_WIDTH), lambda n: (0, 0)),
            pl.BlockSpec((_WO_SLAB, D_MODEL), lambda n: (n, 0)),
        ],
        out_specs=[
            pl.BlockSpec((MIX_ROWS, D_MODEL), lambda n: (n, 0)),
            pl.BlockSpec((_WO_SLAB, D_MODEL), lambda n: (n, 0)),
        ],
        out_shape=[
            jax.ShapeDtypeStruct((SEQ, D_MODEL), BF16),
            jax.ShapeDtypeStruct((D_MODEL, D_MODEL), BF16),
        ],
        scratch_shapes=[
            pltpu.VMEM((3 * N_HEADS, BLOCK, KEYS), F32),
            pltpu.VMEM((N_POOL_GROUPS, MIX_ROWS, _MIX_EXT), BF16),
        ],
        compiler_params=_params("arbitrary"),
    )(sink_logits, proj, proj, proj, proj, proj, proj, proj, proj, proj, proj, pool_w16, pool_scale,
      w_out)


OUTPROJ_BM = 1024
OUTPROJ_BN = 1024
_OUTPROJ_NJ = D_MODEL // OUTPROJ_BN
_OUT_MROWS = 512


def _outproj_kernel(mx_ref, w_ref, x_ref, g2_ref, x1_ref, xg_ref, rs_ref, ss_ref):
    j = pl.program_id(1)

    @pl.when(j == 0)
    def _():
        ss_ref[...] = jnp.zeros_like(ss_ref)

    for r in range(OUTPROJ_BM // _OUT_MROWS):
        rows = slice(r * _OUT_MROWS, (r + 1) * _OUT_MROWS)
        for c in range(OUTPROJ_BN // MXU_N):
            cs = slice(c * MXU_N, (c + 1) * MXU_N)
            x1 = x_ref[rows, cs] + jnp.dot(mx_ref[rows, :], w_ref[:, cs],
                                           preferred_element_type=F32)
            x1_ref[rows, cs] = x1
            xg_ref[rows, cs] = (x1 * g2_ref[:, cs]).astype(BF16)
            sq = x1 * x1
            ss_ref[rows, :] += sum(sq[:, k * LANES:(k + 1) * LANES] for k in range(MXU_N // LANES))

    @pl.when(j == _OUTPROJ_NJ - 1)
    def _():
        ms = jnp.sum(ss_ref[...], axis=-1, keepdims=True) * (1.0 / D_MODEL)
        rs_ref[...] = lax.rsqrt(ms + RMS_EPS)


def _outproj(mixed, w_out16, x, g2):
    return pl.pallas_call(
        _outproj_kernel,
        name="outproj",
        grid=(SEQ // OUTPROJ_BM, _OUTPROJ_NJ),
        in_specs=[
            pl.BlockSpec((OUTPROJ_BM, D_MODEL), lambda i, j: (i, 0)),
            pl.BlockSpec((D_MODEL, OUTPROJ_BN), lambda i, j: (0, j)),
            pl.BlockSpec((OUTPROJ_BM, OUTPROJ_BN), lambda i, j: (i, j)),
            pl.BlockSpec((1, OUTPROJ_BN), lambda i, j: (0, j)),
        ],
        out_specs=[
            pl.BlockSpec((OUTPROJ_BM, OUTPROJ_BN), lambda i, j: (i, j)),
            pl.BlockSpec((OUTPROJ_BM, OUTPROJ_BN), lambda i, j: (i, j)),
            pl.BlockSpec((OUTPROJ_BM, 1), lambda i, j: (i, 0)),
        ],
        out_shape=[
            jax.ShapeDtypeStruct((SEQ, D_MODEL), F32),
            jax.ShapeDtypeStruct((SEQ, D_MODEL), BF16),
            jax.ShapeDtypeStruct((SEQ, 1), F32),
        ],
        scratch_shapes=[pltpu.VMEM((OUTPROJ_BM, LANES), F32)],
        compiler_params=_params("arbitrary", "arbitrary"),
    )(mixed, w_out16, x, g2)


UP_BM = 2048
_UP_MROWS = 1024
FFN_BF = 256
_N_FF_BLOCKS = D_FF // FFN_BF
DOWN_BM = 512
DOWN_BN = 512
_DOWN_K0 = (0, 2816, 5632, 8448)
_DOWN_KLEN = (2816, 2816, 2816, 2560)
_DOWN_KCHUNKS = len(_DOWN_K0)


def _ffn_up_kernel(xg_ref, rs_ref, wg_ref, wu_ref, wd_ref, a_ref, wd16_ref):
    wg = wg_ref[...].astype(BF16)
    wu = wu_ref[...].astype(BF16)
    for r in range(UP_BM // _UP_MROWS):
        rows = slice(r * _UP_MROWS, (r + 1) * _UP_MROWS)
        xg = xg_ref[rows, :]
        rs = rs_ref[rows, :]
        g = jnp.dot(xg, wg, preferred_element_type=F32) * rs
        u = jnp.dot(xg, wu, preferred_element_type=F32) * rs
        a_ref[rows, :] = (g * jax.nn.sigmoid(g) * u).astype(BF16)

    @pl.when(pl.program_id(0) == 0)
    def _():
        wd16_ref[...] = wd_ref[...].astype(BF16)


def _ffn_up(xg, rs2, w_gate, w_up, w_down):
    wd_index = lambda i, f: (jnp.where(i == 0, f, _N_FF_BLOCKS - 1), 0)
    return pl.pallas_call(
        _ffn_up_kernel,
        name="ffn_up",
        grid=(SEQ // UP_BM, _N_FF_BLOCKS),
        in_specs=[
            pl.BlockSpec((UP_BM, D_MODEL), lambda i, f: (i, 0), pipeline_mode=pl.Buffered(1)),
            pl.BlockSpec((UP_BM, 1), lambda i, f: (i, 0), pipeline_mode=pl.Buffered(1)),
            pl.BlockSpec((D_MODEL, FFN_BF), lambda i, f: (0, f)),
            pl.BlockSpec((D_MODEL, FFN_BF), lambda i, f: (0, f)),
            pl.BlockSpec((FFN_BF, D_MODEL), wd_index),
        ],
        out_specs=[
            pl.BlockSpec((UP_BM, FFN_BF), lambda i, f: (i, f)),
            pl.BlockSpec((FFN_BF, D_MODEL), wd_index),
        ],
        out_shape=[
            jax.ShapeDtypeStruct((SEQ, D_FF), BF16),
            jax.ShapeDtypeStruct((D_FF, D_MODEL), BF16),
        ],
        compiler_params=_params("arbitrary", "arbitrary"),
    )(xg, rs2, w_gate, w_up, w_down)


def _ffn_down_kernel(a_hbm, w_ref, x1_ref, o_ref, a_buf, sems):
    i = pl.program_id(0)
    j = pl.program_id(1)
    n_i = pl.num_programs(0)
    n_j = pl.num_programs(1)

    def chunk_copy(row_block, c):
        k0, kl = _DOWN_K0[c], _DOWN_KLEN[c]
        return pltpu.make_async_copy(
            a_hbm.at[pl.ds(row_block * DOWN_BM, DOWN_BM), pl.ds(k0, kl)],
            a_buf.at[row_block % 2, :, pl.ds(k0, kl)], sems.at[c])

    @pl.when((i == 0) & (j == 0))
    def _():
        for c in range(_DOWN_KCHUNKS):
            chunk_copy(0, c).start()

    @pl.when(j == 0)
    def _():
        for c in range(_DOWN_KCHUNKS):
            chunk_copy(i, c).wait()

    for c in range(_DOWN_KCHUNKS):
        @pl.when((j == n_j - _DOWN_KCHUNKS + c) & (i < n_i - 1))
        def _():
            chunk_copy(i + 1, c).start()

    a_blk = a_buf.at[i % 2]
    for c in range(DOWN_BN // MXU_N):
        cs = slice(c * MXU_N, (c + 1) * MXU_N)
        o_ref[:, cs] = x1_ref[:, cs] + jnp.dot(a_blk[...], w_ref[:, cs], preferred_element_type=F32)


def _ffn_down(a, w_down16, x1):
    assert _DOWN_KCHUNKS <= D_MODEL // DOWN_BN
    return pl.pallas_call(
        _ffn_down_kernel,
        name="ffn_down",
        grid=(SEQ // DOWN_BM, D_MODEL // DOWN_BN),
        in_specs=[
            pl.BlockSpec(memory_space=pl.ANY),
            pl.BlockSpec((D_FF, DOWN_BN), lambda i, j: (0, j)),
            pl.BlockSpec((DOWN_BM, DOWN_BN), lambda i, j: (i, j)),
        ],
        out_specs=pl.BlockSpec((DOWN_BM, DOWN_BN), lambda i, j: (i, j)),
        out_shape=jax.ShapeDtypeStruct((SEQ, D_MODEL), F32),
        scratch_shapes=[
            pltpu.VMEM((2, DOWN_BM, D_FF), BF16),
            pltpu.SemaphoreType.DMA((_DOWN_KCHUNKS,)),
        ],
        compiler_params=_params("arbitrary", "arbitrary"),
    )(a, w_down16, x1)


def kernel(x, norm1_g, w_in, q_norm_g, k_norm_g, sink_logits, pool_w, pool_scale, w_out, norm2_g,
           w_gate, w_up, w_down):
    b, s, d = x.shape
    assert (b, s, d) == (1, SEQ, D_MODEL)
    x2 = x.reshape(s, d)
    gain = jnp.concatenate([
        jnp.tile(q_norm_g, N_HEADS) * (Q_SCALE * LOG2E),
        jnp.ones((POOL_WIDTH,), F32),
        jnp.tile(k_norm_g, N_KV_HEADS),
        jnp.ones((KV_WIDTH,), F32),
    ]).reshape(1, IN_COLS)
    g1 = norm1_g.reshape(1, D_MODEL)
    proj_head, w_in16 = _inproj_head(x2, g1, w_in, gain)
    proj = _inproj(x2, g1, w_in16, gain, proj_head)
    mixed, w_out16 = _mixers(sink_logits, proj, pool_w.astype(BF16),
                             pool_scale.reshape(1, POOL_WIDTH), w_out)
    x1, xg, rs2 = _outproj(mixed, w_out16, x2, norm2_g.reshape(1, D_MODEL))
    a, w_down16 = _ffn_up(xg, rs2, w_gate, w_up, w_down)
    out = _ffn_down(a, w_down16, x1)
    return out.reshape(b, s, d)
```

```python
import jax
import jax.numpy as jnp
from jax import lax
from jax.experimental import pallas as pl
from jax.experimental.pallas import tpu as pltpu

F32 = jnp.float32
BF16 = jnp.bfloat16

D_MODEL = 4096
SEQ = 8192
HEAD_DIM = 128
N_HEADS = 16
N_KV_HEADS = 4
GROUP = N_HEADS // N_KV_HEADS
ATTN_WIDTH = N_HEADS * HEAD_DIM
KV_WIDTH = N_KV_HEADS * HEAD_DIM
POOL_WIDTH = D_MODEL - ATTN_WIDTH
POOL_WINDOWS = (2, 4, 8, 16)
N_POOL_GROUPS = len(POOL_WINDOWS)
POOL_GROUP_W = POOL_WIDTH // N_POOL_GROUPS
IN_COLS = ATTN_WIDTH + 2 * KV_WIDTH + POOL_WIDTH
D_FF = 11008
WINDOW = 128
BLOCK = 128
RMS_EPS = 1e-6
NEG_INF = -1e30
Q_SCALE = HEAD_DIM ** -0.5
LOG2E = 1.4426950408889634
ALIBI_SLOPES = tuple(2.0 ** (-8.0 * h / N_HEADS) for h in range(1, N_HEADS + 1))

VMEM_LIMIT_BYTES = 56 * 1024 * 1024
MXU_N = 256
LANES = 128
POOL_HALO = 16


def _params(*semantics):
    return pltpu.CompilerParams(dimension_semantics=semantics, vmem_limit_bytes=VMEM_LIMIT_BYTES)


INPROJ_BM = 1024
INPROJ_BN = 1024
_IN_NI = SEQ // INPROJ_BM
_IN_NJ = IN_COLS // INPROJ_BN
_IN_SUBS = 4
_IN_SUB_ROWS = INPROJ_BM // _IN_SUBS
_IN_MROWS = 512
_IN_STAGE_COLS = 512
_HEAD_BLOCKS = 2
HEAD_ROWS = _HEAD_BLOCKS * INPROJ_BM
_HEAD_STAGE_STEPS = HEAD_ROWS // _IN_SUB_ROWS
_Q_GROUPS = ATTN_WIDTH // HEAD_DIM
_K_GROUP0 = (ATTN_WIDTH + POOL_WIDTH) // HEAD_DIM
_K_GROUPS = N_KV_HEADS


def _w_col(j, bn):
    q_blocks, p_blocks, kv_blocks = ATTN_WIDTH // bn, POOL_WIDTH // bn, 2 * KV_WIDTH // bn
    assert q_blocks * bn == ATTN_WIDTH and p_blocks * bn == POOL_WIDTH and kv_blocks * bn == 2 * KV_WIDTH
    return jnp.where(j < q_blocks, j, jnp.where(j < q_blocks + p_blocks, j + kv_blocks, j - p_blocks))


def _stage_rows(x_ref, g1_ref, h_ref, rs_ref, row0, n_rows):
    sq_lanes = jnp.zeros((n_rows, LANES), F32)
    for k in range(D_MODEL // _IN_STAGE_COLS):
        cs = slice(k * _IN_STAGE_COLS, (k + 1) * _IN_STAGE_COLS)
        x = x_ref[:, cs]
        sq = x * x
        sq_lanes = sq_lanes + sum(
            sq[:, t * LANES:(t + 1) * LANES] for t in range(_IN_STAGE_COLS // LANES))
        h_ref[pl.ds(row0, n_rows), cs] = (x * g1_ref[:, cs]).astype(BF16)
    ms = jnp.sum(sq_lanes, axis=-1, keepdims=True) * (1.0 / D_MODEL)
    rs_ref[pl.ds(row0, n_rows), :] = lax.rsqrt(ms + RMS_EPS)


def _store_groups(acc, first_group, gain_ref, o_ref, rows, col0):
    for hh in range(MXU_N // HEAD_DIM):
        col = col0 + hh * HEAD_DIM
        group = first_group + hh
        blk = acc[:, hh * HEAD_DIM:(hh + 1) * HEAD_DIM]
        ms = jnp.mean(blk * blk, axis=-1, keepdims=True)
        normed = (group < _Q_GROUPS) | ((group >= _K_GROUP0) & (group < _K_GROUP0 + _K_GROUPS))
        scale = jnp.where(normed, lax.rsqrt(ms + RMS_EPS), 1.0)
        o_ref[rows, col:col + HEAD_DIM] = (
            blk * scale * gain_ref[:, col:col + HEAD_DIM]).astype(BF16)


def _inproj_head_kernel(x_ref, g1_ref, w_ref, gain_ref, o_ref, w16_ref, h_ref, rs_ref):
    t = pl.program_id(0)

    @pl.when(t < _HEAD_STAGE_STEPS)
    def _():
        row0 = pl.multiple_of(t * _IN_SUB_ROWS, _IN_SUB_ROWS)
        _stage_rows(x_ref, g1_ref, h_ref, rs_ref, row0, _IN_SUB_ROWS)

    @pl.when(t >= _HEAD_STAGE_STEPS)
    def _():
        j = t - _HEAD_STAGE_STEPS
        w16 = w_ref[...].astype(BF16)
        w16_ref[...] = w16
        for r in range(HEAD_ROWS // _IN_MROWS):
            rows = slice(r * _IN_MROWS, (r + 1) * _IN_MROWS)
            acc = jnp.dot(h_ref[rows, :], w16, preferred_element_type=F32) * rs_ref[rows, :]
            _store_groups(acc, j * (MXU_N // HEAD_DIM), gain_ref, o_ref, rows, 0)


def _inproj_head(x, g1, w_in, gain):
    col = lambda t: jnp.maximum(t - _HEAD_STAGE_STEPS, 0)
    w_index = lambda t: (0, _w_col(col(t), MXU_N))
    return pl.pallas_call(
        _inproj_head_kernel,
        name="inproj_head",
        grid=(_HEAD_STAGE_STEPS + IN_COLS // MXU_N,),
        in_specs=[
            pl.BlockSpec((_IN_SUB_ROWS, D_MODEL), lambda t: (jnp.minimum(t, _HEAD_STAGE_STEPS - 1), 0)),
            pl.BlockSpec((1, D_MODEL), lambda t: (0, 0)),
            pl.BlockSpec((D_MODEL, MXU_N), w_index),
            pl.BlockSpec((1, MXU_N), lambda t: (0, col(t))),
        ],
        out_specs=[
            pl.BlockSpec((HEAD_ROWS, MXU_N), lambda t: (0, col(t))),
            pl.BlockSpec((D_MODEL, MXU_N), w_index),
        ],
        out_shape=[
            jax.ShapeDtypeStruct((HEAD_ROWS, IN_COLS), BF16),
            jax.ShapeDtypeStruct((D_MODEL, IN_COLS), BF16),
        ],
        scratch_shapes=[pltpu.VMEM((HEAD_ROWS, D_MODEL), BF16), pltpu.VMEM((HEAD_ROWS, 1), F32)],
        compiler_params=_params("arbitrary"),
    )(x, g1, w_in, gain)


def _inproj_kernel(x_ref, g1_ref, w_ref, gain_ref, head_ref, o_ref, h_a, rs_a, h_b, rs_b):
    i = pl.program_id(0)
    j = pl.program_id(1)
    row0 = pl.multiple_of(jnp.minimum(j, _IN_SUBS - 1) * _IN_SUB_ROWS, _IN_SUB_ROWS)

    def stage_rows(h_ref, rs_ref):
        _stage_rows(x_ref, g1_ref, h_ref, rs_ref, row0, _IN_SUB_ROWS)

    def project(h_ref, rs_ref):
        for c in range(INPROJ_BN // MXU_N):
            for r in range(INPROJ_BM // _IN_MROWS):
                rows = slice(r * _IN_MROWS, (r + 1) * _IN_MROWS)
                acc = jnp.dot(h_ref[rows, :], w_ref[:, c * MXU_N:(c + 1) * MXU_N],
                              preferred_element_type=F32) * rs_ref[rows, :]
                first_group = j * (INPROJ_BN // HEAD_DIM) + c * (MXU_N // HEAD_DIM)
                _store_groups(acc, first_group, gain_ref, o_ref, rows, c * MXU_N)

    own = i >= _HEAD_BLOCKS
    from_a = (i - _HEAD_BLOCKS) % 2 == 0

    @pl.when(i < _HEAD_BLOCKS)
    def _():
        o_ref[...] = head_ref[...]

    @pl.when(i == _HEAD_BLOCKS - 1)
    def _():
        stage_rows(h_a, rs_a)

    @pl.when(own & from_a)
    def _():
        stage_rows(h_b, rs_b)
        project(h_a, rs_a)

    @pl.when(own & jnp.logical_not(from_a))
    def _():
        stage_rows(h_a, rs_a)
        project(h_b, rs_b)


def _inproj(x, g1, w_in16, gain, proj_head):
    assert _IN_SUBS <= _IN_NJ
    staging = lambda i: i >= _HEAD_BLOCKS - 1
    x_index = lambda i, j: (
        jnp.clip(i + 1, _HEAD_BLOCKS, _IN_NI - 1) * _IN_SUBS
        + jnp.where(staging(i), jnp.minimum(j, _IN_SUBS - 1), 0), 0)
    pinned = lambda i, j: jnp.where(i < _HEAD_BLOCKS, 0, j)
    return pl.pallas_call(
        _inproj_kernel,
        name="inproj",
        grid=(_IN_NI, _IN_NJ),
        in_specs=[
            pl.BlockSpec((_IN_SUB_ROWS, D_MODEL), x_index),
            pl.BlockSpec((1, D_MODEL), lambda i, j: (0, 0)),
            pl.BlockSpec((D_MODEL, INPROJ_BN), lambda i, j: (0, _w_col(pinned(i, j), INPROJ_BN))),
            pl.BlockSpec((1, INPROJ_BN), lambda i, j: (0, pinned(i, j))),
            pl.BlockSpec((INPROJ_BM, INPROJ_BN),
                         lambda i, j: (jnp.minimum(i, _HEAD_BLOCKS - 1),
                                       jnp.where(i < _HEAD_BLOCKS, j, _IN_NJ - 1))),
        ],
        out_specs=pl.BlockSpec((INPROJ_BM, INPROJ_BN), lambda i, j: (i, j)),
        out_shape=jax.ShapeDtypeStruct((SEQ, IN_COLS), BF16),
        scratch_shapes=[
            pltpu.VMEM((INPROJ_BM, D_MODEL), BF16), pltpu.VMEM((INPROJ_BM, 1), F32),
            pltpu.VMEM((INPROJ_BM, D_MODEL), BF16), pltpu.VMEM((INPROJ_BM, 1), F32),
        ],
        compiler_params=_params("arbitrary", "arbitrary"),
    )(x, g1, w_in16, gain, proj_head)


MIX_ROWS = 2 * BLOCK
_MIX_STEPS = SEQ // MIX_ROWS
_MIX_EXT = MIX_ROWS + 2 * POOL_HALO
_WO_SLAB = D_MODEL // _MIX_STEPS
KEYS = 3 * BLOCK
_P_COLBLOCK = ATTN_WIDTH // POOL_WIDTH
_K_COLBLOCK = (ATTN_WIDTH + POOL_WIDTH) // KV_WIDTH
_V_COLBLOCK = _K_COLBLOCK + 1


def _mixers_kernel(sink_ref, q_ref, kp_ref, kc_ref, kn_ref, vp_ref, vc_ref, vn_ref,
                   pp_ref, pc_ref, pn_ref, pw_ref, sc_ref, wo_ref, o_ref, wo16_ref,
                   bias_ref, band_ref):
    n = pl.program_id(0)
    wo16_ref[...] = wo_ref[...].astype(BF16)

    @pl.when(n == 0)
    def _():
        qi = lax.broadcasted_iota(jnp.int32, (BLOCK, KEYS), 0)
        kj = lax.broadcasted_iota(jnp.int32, (BLOCK, KEYS), 1)
        dist = jnp.abs(kj - BLOCK - qi)
        in_band = dist <= WINDOW
        distf = dist.astype(F32)
        for v, valid in enumerate((in_band, in_band & (kj >= BLOCK), in_band & (kj < 2 * BLOCK))):
            for h in range(N_HEADS):
                bias_ref[v * N_HEADS + h] = jnp.where(
                    valid, (-ALIBI_SLOPES[h] * LOG2E) * distf, NEG_INF)

        t = lax.broadcasted_iota(jnp.int32, (MIX_ROWS, _MIX_EXT), 0)
        s_rel = lax.broadcasted_iota(jnp.int32, (MIX_ROWS, _MIX_EXT), 1) - POOL_HALO
        for gi, w in enumerate(POOL_WINDOWS):
            left = w // 2
            right = w - 1 - left
            band_ref[gi] = jnp.where((s_rel >= t - left) & (s_rel <= t + right), 1.0, 0.0).astype(BF16)

    is_first = n == 0
    is_last = n == _MIX_STEPS - 1
    row_head = lax.broadcasted_iota(jnp.int32, (GROUP * BLOCK, 1), 0) // BLOCK
    ones_kd = jnp.ones((KEYS, HEAD_DIM), BF16)

    p_cur = pc_ref[...]
    p_prev = jnp.where(is_first, jnp.zeros_like(pp_ref), pp_ref[...])
    p_next = jnp.where(is_last, jnp.zeros_like(pn_ref), pn_ref[...])
    p_ext = jnp.concatenate([p_prev, p_cur, p_next], axis=0)
    tg = n * MIX_ROWS + lax.broadcasted_iota(jnp.int32, (MIX_ROWS, 1), 0)

    def pool_group(gi):
        w = POOL_WINDOWS[gi]
        left = w // 2
        right = w - 1 - left
        cs = slice(gi * POOL_GROUP_W, (gi + 1) * POOL_GROUP_W)
        win_sum = jnp.dot(band_ref[gi], p_ext[:, cs], preferred_element_type=F32)
        cnt = jnp.minimum(tg + right + 1, SEQ) - jnp.maximum(tg - left, 0)
        u = win_sum * (1.0 / cnt.astype(F32)) - p_cur[:, cs].astype(F32)
        y = jnp.dot(u.astype(BF16), pw_ref[gi], preferred_element_type=F32)
        o_ref[:, ATTN_WIDTH + gi * POOL_GROUP_W:ATTN_WIDTH + (gi + 1) * POOL_GROUP_W] = (
            y * sc_ref[:, cs]).astype(BF16)

    assert N_POOL_GROUPS == N_KV_HEADS
    for kh in range(N_KV_HEADS):
        ksl = slice(kh * HEAD_DIM, (kh + 1) * HEAD_DIM)
        k_all = jnp.concatenate([kp_ref[:, ksl], kc_ref[:, ksl], kn_ref[:, ksl]], axis=0)
        v_all = jnp.concatenate([vp_ref[:, ksl], vc_ref[:, ksl], vn_ref[:, ksl]], axis=0)
        h0 = kh * GROUP
        sink = jnp.full((GROUP * BLOCK, 1), sink_ref[h0 + GROUP - 1] * LOG2E, F32)
        for g in range(GROUP - 1):
            sink = jnp.where(row_head == g, sink_ref[h0 + g] * LOG2E, sink)
        for qb in range(MIX_ROWS // BLOCK):
            qrows = slice(qb * BLOCK, (qb + 1) * BLOCK)
            variant = jnp.where(is_first, 1, 0) if qb == 0 else jnp.where(is_last, 2, 0)
            kw = k_all[qb * BLOCK:qb * BLOCK + KEYS]
            vw = v_all[qb * BLOCK:qb * BLOCK + KEYS]
            qs = jnp.concatenate(
                [q_ref[qrows, (h0 + g) * HEAD_DIM:(h0 + g + 1) * HEAD_DIM] for g in range(GROUP)],
                axis=0)
            s = lax.dot_general(qs, kw, (((1,), (1,)), ((), ())), preferred_element_type=F32)
            bias = bias_ref[pl.ds(variant * N_HEADS + h0, GROUP)].reshape(GROUP * BLOCK, KEYS)
            logits = s + bias
            m = jnp.maximum(jnp.max(logits, axis=-1, keepdims=True), sink)
            e = jnp.exp2(logits - m)
            ov = jnp.dot(e.astype(BF16), jnp.concatenate([vw, ones_kd], axis=1),
                         preferred_element_type=F32)
            o = ov[:, :HEAD_DIM] * (1.0 / (ov[:, HEAD_DIM:] + jnp.exp2(sink - m)))
            for g in range(GROUP):
                o_ref[qrows, (h0 + g) * HEAD_DIM:(h0 + g + 1) * HEAD_DIM] = (
                    o[g * BLOCK:(g + 1) * BLOCK].astype(BF16))
        pool_group(kh)


def _mixers(sink_logits, proj, pool_w16, pool_scale, w_out):
    last_qblock = SEQ // BLOCK - 1
    last_halo = SEQ // POOL_HALO - 1
    qb_per_step = MIX_ROWS // BLOCK
    halo_per_step = MIX_ROWS // POOL_HALO
    kv_prev = lambda col: pl.BlockSpec(
        (BLOCK, KV_WIDTH), lambda n: (jnp.maximum(n * qb_per_step - 1, 0), col))
    kv_cur = lambda col: pl.BlockSpec((MIX_ROWS, KV_WIDTH), lambda n: (n, col))
    kv_next = lambda col: pl.BlockSpec(
        (BLOCK, KV_WIDTH), lambda n: (jnp.minimum((n + 1) * qb_per_step, last_qblock), col))
    return pl.pallas_call(
        _mixers_kernel,
        name="mixers",
        grid=(_MIX_STEPS,),
        in_specs=[
            pl.BlockSpec(memory_space=pltpu.SMEM),
            pl.BlockSpec((MIX_ROWS, ATTN_WIDTH), lambda n: (n, 0)),
            kv_prev(_K_COLBLOCK), kv_cur(_K_COLBLOCK), kv_next(_K_COLBLOCK),
            kv_prev(_V_COLBLOCK), kv_cur(_V_COLBLOCK), kv_next(_V_COLBLOCK),
            pl.BlockSpec((POOL_HALO, POOL_WIDTH),
                         lambda n: (jnp.maximum(n * halo_per_step - 1, 0), _P_COLBLOCK)),
            pl.BlockSpec((MIX_ROWS, POOL_WIDTH), lambda n: (n, _P_COLBLOCK)),
            pl.BlockSpec((POOL_HALO, POOL_WIDTH),
                         lambda n: (jnp.minimum((n + 1) * halo_per_step, last_halo), _P_COLBLOCK)),
            pl.BlockSpec((N_POOL_GROUPS, POOL_GROUP_W, POOL_GROUP_W), lambda n: (0, 0, 0)),
            pl.BlockSpec((1, POOL_WIDTH), lambda n: (0, 0)),
            pl.BlockSpec((_WO_SLAB, D_MODEL), lambda n: (n, 0)),
        ],
        out_specs=[
            pl.BlockSpec((MIX_ROWS, D_MODEL), lambda n: (n, 0)),
            pl.BlockSpec((_WO_SLAB, D_MODEL), lambda n: (n, 0)),
        ],
        out_shape=[
            jax.ShapeDtypeStruct((SEQ, D_MODEL), BF16),
            jax.ShapeDtypeStruct((D_MODEL, D_MODEL), BF16),
        ],
        scratch_shapes=[
            pltpu.VMEM((3 * N_HEADS, BLOCK, KEYS), F32),
            pltpu.VMEM((N_POOL_GROUPS, MIX_ROWS, _MIX_EXT), BF16),
        ],
        compiler_params=_params("arbitrary"),
    )(sink_logits, proj, proj, proj, proj, proj, proj, proj, proj, proj, proj, pool_w16, pool_scale,
      w_out)


OUTPROJ_BM = 1024
OUTPROJ_BN = 1024
_OUTPROJ_NJ = D_MODEL // OUTPROJ_BN
_OUT_MROWS = 512


def _outproj_kernel(mx_ref, w_ref, x_ref, g2_ref, x1_ref, xg_ref, rs_ref, ss_ref):
    j = pl.program_id(1)

    @pl.when(j == 0)
    def _():
        ss_ref[...] = jnp.zeros_like(ss_ref)

    for r in range(OUTPROJ_BM // _OUT_MROWS):
        rows = slice(r * _OUT_MROWS, (r + 1) * _OUT_MROWS)
        for c in range(OUTPROJ_BN // MXU_N):
            cs = slice(c * MXU_N, (c + 1) * MXU_N)
            x1 = x_ref[rows, cs] + jnp.dot(mx_ref[rows, :], w_ref[:, cs],
                                           preferred_element_type=F32)
            x1_ref[rows, cs] = x1
            xg_ref[rows, cs] = (x1 * g2_ref[:, cs]).astype(BF16)
            sq = x1 * x1
            ss_ref[rows, :] += sum(sq[:, k * LANES:(k + 1) * LANES] for k in range(MXU_N // LANES))

    @pl.when(j == _OUTPROJ_NJ - 1)
    def _():
        ms = jnp.sum(ss_ref[...], axis=-1, keepdims=True) * (1.0 / D_MODEL)
        rs_ref[...] = lax.rsqrt(ms + RMS_EPS)


def _outproj(mixed, w_out16, x, g2):
    return pl.pallas_call(
        _outproj_kernel,
        name="outproj",
        grid=(SEQ // OUTPROJ_BM, _OUTPROJ_NJ),
        in_specs=[
            pl.BlockSpec((OUTPROJ_BM, D_MODEL), lambda i, j: (i, 0)),
            pl.BlockSpec((D_MODEL, OUTPROJ_BN), lambda i, j: (0, j)),
            pl.BlockSpec((OUTPROJ_BM, OUTPROJ_BN), lambda i, j: (i, j)),
            pl.BlockSpec((1, OUTPROJ_BN), lambda i, j: (0, j)),
        ],
        out_specs=[
            pl.BlockSpec((OUTPROJ_BM, OUTPROJ_BN), lambda i, j: (i, j)),
            pl.BlockSpec((OUTPROJ_BM, OUTPROJ_BN), lambda i, j: (i, j)),
            pl.BlockSpec((OUTPROJ_BM, 1), lambda i, j: (i, 0)),
        ],
        out_shape=[
            jax.ShapeDtypeStruct((SEQ, D_MODEL), F32),
            jax.ShapeDtypeStruct((SEQ, D_MODEL), BF16),
            jax.ShapeDtypeStruct((SEQ, 1), F32),
        ],
        scratch_shapes=[pltpu.VMEM((OUTPROJ_BM, LANES), F32)],
        compiler_params=_params("arbitrary", "arbitrary"),
    )(mixed, w_out16, x, g2)


UP_BM = 2048
_UP_MROWS = 1024
FFN_BF = 256
_N_FF_BLOCKS = D_FF // FFN_BF
DOWN_BM = 512
DOWN_BN = 512
_DOWN_K0 = (0, 2816, 5632, 8448)
_DOWN_KLEN = (2816, 2816, 2816, 2560)
_DOWN_KCHUNKS = len(_DOWN_K0)


def _ffn_up_kernel(xg_ref, rs_ref, wg_ref, wu_ref, wd_ref, a_ref, wd16_ref):
    wg = wg_ref[...].astype(BF16)
    wu = wu_ref[...].astype(BF16)
    for r in range(UP_BM // _UP_MROWS):
        rows = slice(r * _UP_MROWS, (r + 1) * _UP_MROWS)
        xg = xg_ref[rows, :]
        rs = rs_ref[rows, :]
        g = jnp.dot(xg, wg, preferred_element_type=F32) * rs
        u = jnp.dot(xg, wu, preferred_element_type=F32) * rs
        a_ref[rows, :] = (g * jax.nn.sigmoid(g) * u).astype(BF16)

    @pl.when(pl.program_id(0) == 0)
    def _():
        wd16_ref[...] = wd_ref[...].astype(BF16)


def _ffn_up(xg, rs2, w_gate, w_up, w_down):
    wd_index = lambda i, f: (jnp.where(i == 0, f, _N_FF_BLOCKS - 1), 0)
    return pl.pallas_call(
        _ffn_up_kernel,
        name="ffn_up",
        grid=(SEQ // UP_BM, _N_FF_BLOCKS),
        in_specs=[
            pl.BlockSpec((UP_BM, D_MODEL), lambda i, f: (i, 0), pipeline_mode=pl.Buffered(1)),
            pl.BlockSpec((UP_BM, 1), lambda i, f: (i, 0), pipeline_mode=pl.Buffered(1)),
            pl.BlockSpec((D_MODEL, FFN_BF), lambda i, f: (0, f)),
            pl.BlockSpec((D_MODEL, FFN_BF), lambda i, f: (0, f)),
            pl.BlockSpec((FFN_BF, D_MODEL), wd_index),
        ],
        out_specs=[
            pl.BlockSpec((UP_BM, FFN_BF), lambda i, f: (i, f)),
            pl.BlockSpec((FFN_BF, D_MODEL), wd_index),
        ],
        out_shape=[
            jax.ShapeDtypeStruct((SEQ, D_FF), BF16),
            jax.ShapeDtypeStruct((D_FF, D_MODEL), BF16),
        ],
        compiler_params=_params("arbitrary", "arbitrary"),
    )(xg, rs2, w_gate, w_up, w_down)


def _ffn_down_kernel(a_hbm, w_ref, x1_ref, o_ref, a_buf, sems):
    i = pl.program_id(0)
    j = pl.program_id(1)
    n_i = pl.num_programs(0)
    n_j = pl.num_programs(1)

    def chunk_copy(row_block, c):
        k0, kl = _DOWN_K0[c], _DOWN_KLEN[c]
        return pltpu.make_async_copy(
            a_hbm.at[pl.ds(row_block * DOWN_BM, DOWN_BM), pl.ds(k0, kl)],
            a_buf.at[row_block % 2, :, pl.ds(k0, kl)], sems.at[c])

    @pl.when((i == 0) & (j == 0))
    def _():
        for c in range(_DOWN_KCHUNKS):
            chunk_copy(0, c).start()

    @pl.when(j == 0)
    def _():
        for c in range(_DOWN_KCHUNKS):
            chunk_copy(i, c).wait()

    for c in range(_DOWN_KCHUNKS):
        @pl.when((j == n_j - _DOWN_KCHUNKS + c) & (i < n_i - 1))
        def _():
            chunk_copy(i + 1, c).start()

    a_blk = a_buf.at[i % 2]
    for c in range(DOWN_BN // MXU_N):
        cs = slice(c * MXU_N, (c + 1) * MXU_N)
        o_ref[:, cs] = x1_ref[:, cs] + jnp.dot(a_blk[...], w_ref[:, cs], preferred_element_type=F32)


def _ffn_down(a, w_down16, x1):
    assert _DOWN_KCHUNKS <= D_MODEL // DOWN_BN
    return pl.pallas_call(
        _ffn_down_kernel,
        name="ffn_down",
        grid=(SEQ // DOWN_BM, D_MODEL // DOWN_BN),
        in_specs=[
            pl.BlockSpec(memory_space=pl.ANY),
            pl.BlockSpec((D_FF, DOWN_BN), lambda i, j: (0, j)),
            pl.BlockSpec((DOWN_BM, DOWN_BN), lambda i, j: (i, j)),
        ],
        out_specs=pl.BlockSpec((DOWN_BM, DOWN_BN), lambda i, j: (i, j)),
        out_shape=jax.ShapeDtypeStruct((SEQ, D_MODEL), F32),
        scratch_shapes=[
            pltpu.VMEM((2, DOWN_BM, D_FF), BF16),
            pltpu.SemaphoreType.DMA((_DOWN_KCHUNKS,)),
        ],
        compiler_params=_params("arbitrary", "arbitrary"),
    )(a, w_down16, x1)


def kernel(x, norm1_g, w_in, q_norm_g, k_norm_g, sink_logits, pool_w, pool_scale, w_out, norm2_g,
           w_gate, w_up, w_down):
    b, s, d = x.shape
    assert (b, s, d) == (1, SEQ, D_MODEL)
    x2 = x.reshape(s, d)
    gain = jnp.concatenate([
        jnp.tile(q_norm_g, N_HEADS) * (Q_SCALE * LOG2E),
        jnp.ones((POOL_WIDTH,), F32),
        jnp.tile(k_norm_g, N_KV_HEADS),
        jnp.ones((KV_WIDTH,), F32),
    ]).reshape(1, IN_COLS)
    g1 = norm1_g.reshape(1, D_MODEL)
    proj_head, w_in16 = _inproj_head(x2, g1, w_in, gain)
    proj = _inproj(x2, g1, w_in16, gain, proj_head)
    mixed, w_out16 = _mixers(sink_logits, proj, pool_w.astype(BF16),
                             pool_scale.reshape(1, POOL_WIDTH), w_out)
    x1, xg, rs2 = _outproj(mixed, w_out16, x2, norm2_g.reshape(1, D_MODEL))
    a, w_down16 = _ffn_up(xg, rs2, w_gate, w_up, w_down)
    out = _ffn_down(a, w_down16, x1)
    return out.reshape(b, s, d)
```

```python
import jax
import jax.numpy as jnp
from jax import lax
from jax.experimental import pallas as pl
from jax.experimental.pallas import tpu as pltpu

F32 = jnp.float32
BF16 = jnp.bfloat16

D_MODEL = 4096
SEQ = 8192
HEAD_DIM = 128
N_HEADS = 16
N_KV_HEADS = 4
GROUP = N_HEADS // N_KV_HEADS
ATTN_WIDTH = N_HEADS * HEAD_DIM
KV_WIDTH = N_KV_HEADS * HEAD_DIM
POOL_WIDTH = D_MODEL - ATTN_WIDTH
POOL_WINDOWS = (2, 4, 8, 16)
N_POOL_GROUPS = len(POOL_WINDOWS)
POOL_GROUP_W = POOL_WIDTH // N_POOL_GROUPS
IN_COLS = ATTN_WIDTH + 2 * KV_WIDTH + POOL_WIDTH
D_FF = 11008
WINDOW = 128
BLOCK = 128
RMS_EPS = 1e-6
NEG_INF = -1e30
Q_SCALE = HEAD_DIM ** -0.5
LOG2E = 1.4426950408889634
ALIBI_SLOPES = tuple(2.0 ** (-8.0 * h / N_HEADS) for h in range(1, N_HEADS + 1))

VMEM_LIMIT_BYTES = 56 * 1024 * 1024
MXU_N = 256
LANES = 128
POOL_HALO = 16


def _params(*semantics):
    return pltpu.CompilerParams(dimension_semantics=semantics, vmem_limit_bytes=VMEM_LIMIT_BYTES)


INPROJ_BM = 1024
INPROJ_BN = 1024
_IN_NI = SEQ // INPROJ_BM
_IN_NJ = IN_COLS // INPROJ_BN
_IN_SUBS = 4
_IN_SUB_ROWS = INPROJ_BM // _IN_SUBS
_IN_MROWS = 512
_IN_STAGE_COLS = 512
_HEAD_BN = 512
_Q_GROUPS = ATTN_WIDTH // HEAD_DIM
_K_GROUP0 = (ATTN_WIDTH + POOL_WIDTH) // HEAD_DIM
_K_GROUPS = N_KV_HEADS


def _w_col(j, bn):
    q_blocks, p_blocks, kv_blocks = ATTN_WIDTH // bn, POOL_WIDTH // bn, 2 * KV_WIDTH // bn
    assert q_blocks * bn == ATTN_WIDTH and p_blocks * bn == POOL_WIDTH and kv_blocks * bn == 2 * KV_WIDTH
    return jnp.where(j < q_blocks, j, jnp.where(j < q_blocks + p_blocks, j + kv_blocks, j - p_blocks))


def _stage_rows(x_ref, g1_ref, h_ref, rs_ref, row0, n_rows):
    sq_lanes = jnp.zeros((n_rows, LANES), F32)
    for k in range(D_MODEL // _IN_STAGE_COLS):
        cs = slice(k * _IN_STAGE_COLS, (k + 1) * _IN_STAGE_COLS)
        x = x_ref[:, cs]
        sq = x * x
        sq_lanes = sq_lanes + sum(
            sq[:, t * LANES:(t + 1) * LANES] for t in range(_IN_STAGE_COLS // LANES))
        h_ref[pl.ds(row0, n_rows), cs] = (x * g1_ref[:, cs]).astype(BF16)
    ms = jnp.sum(sq_lanes, axis=-1, keepdims=True) * (1.0 / D_MODEL)
    rs_ref[pl.ds(row0, n_rows), :] = lax.rsqrt(ms + RMS_EPS)


def _store_groups(acc, first_group, gain_ref, o_ref, rows, col0):
    for hh in range(MXU_N // HEAD_DIM):
        col = col0 + hh * HEAD_DIM
        group = first_group + hh
        blk = acc[:, hh * HEAD_DIM:(hh + 1) * HEAD_DIM]
        ms = jnp.mean(blk * blk, axis=-1, keepdims=True)
        normed = (group < _Q_GROUPS) | ((group >= _K_GROUP0) & (group < _K_GROUP0 + _K_GROUPS))
        scale = jnp.where(normed, lax.rsqrt(ms + RMS_EPS), 1.0)
        o_ref[rows, col:col + HEAD_DIM] = (
            blk * scale * gain_ref[:, col:col + HEAD_DIM]).astype(BF16)


def _inproj_head_kernel(x_ref, g1_ref, w_ref, gain_ref, o_ref, w16_ref, h_ref, rs_ref):
    t = pl.program_id(0)

    @pl.when(t < _IN_SUBS)
    def _():
        row0 = pl.multiple_of(t * _IN_SUB_ROWS, _IN_SUB_ROWS)
        _stage_rows(x_ref, g1_ref, h_ref, rs_ref, row0, _IN_SUB_ROWS)

    @pl.when(t >= _IN_SUBS)
    def _():
        j = t - _IN_SUBS
        w16 = w_ref[...].astype(BF16)
        w16_ref[...] = w16
        for c in range(_HEAD_BN // MXU_N):
            for r in range(INPROJ_BM // _IN_MROWS):
                rows = slice(r * _IN_MROWS, (r + 1) * _IN_MROWS)
                acc = jnp.dot(h_ref[rows, :], w16[:, c * MXU_N:(c + 1) * MXU_N],
                              preferred_element_type=F32) * rs_ref[rows, :]
                first_group = j * (_HEAD_BN // HEAD_DIM) + c * (MXU_N // HEAD_DIM)
                _store_groups(acc, first_group, gain_ref, o_ref, rows, c * MXU_N)


def _inproj_head(x, g1, w_in, gain):
    col = lambda t: jnp.maximum(t - _IN_SUBS, 0)
    w_index = lambda t: (0, _w_col(col(t), _HEAD_BN))
    return pl.pallas_call(
        _inproj_head_kernel,
        name="inproj_head",
        grid=(_IN_SUBS + IN_COLS // _HEAD_BN,),
        in_specs=[
            pl.BlockSpec((_IN_SUB_ROWS, D_MODEL), lambda t: (jnp.minimum(t, _IN_SUBS - 1), 0)),
            pl.BlockSpec((1, D_MODEL), lambda t: (0, 0)),
            pl.BlockSpec((D_MODEL, _HEAD_BN), w_index),
            pl.BlockSpec((1, _HEAD_BN), lambda t: (0, col(t))),
        ],
        out_specs=[
            pl.BlockSpec((INPROJ_BM, _HEAD_BN), lambda t: (0, col(t))),
            pl.BlockSpec((D_MODEL, _HEAD_BN), w_index),
        ],
        out_shape=[
            jax.ShapeDtypeStruct((INPROJ_BM, IN_COLS), BF16),
            jax.ShapeDtypeStruct((D_MODEL, IN_COLS), BF16),
        ],
        scratch_shapes=[pltpu.VMEM((INPROJ_BM, D_MODEL), BF16), pltpu.VMEM((INPROJ_BM, 1), F32)],
        compiler_params=_params("arbitrary"),
    )(x, g1, w_in, gain)


def _inproj_kernel(x_ref, g1_ref, w_ref, gain_ref, head_ref, o_ref, h_a, rs_a, h_b, rs_b):
    i = pl.program_id(0)
    j = pl.program_id(1)
    row0 = pl.multiple_of(jnp.minimum(j, _IN_SUBS - 1) * _IN_SUB_ROWS, _IN_SUB_ROWS)

    def stage_rows(h_ref, rs_ref):
        _stage_rows(x_ref, g1_ref, h_ref, rs_ref, row0, _IN_SUB_ROWS)

    def project(h_ref, rs_ref):
        for c in range(INPROJ_BN // MXU_N):
            for r in range(INPROJ_BM // _IN_MROWS):
                rows = slice(r * _IN_MROWS, (r + 1) * _IN_MROWS)
                acc = jnp.dot(h_ref[rows, :], w_ref[:, c * MXU_N:(c + 1) * MXU_N],
                              preferred_element_type=F32) * rs_ref[rows, :]
                first_group = j * (INPROJ_BN // HEAD_DIM) + c * (MXU_N // HEAD_DIM)
                _store_groups(acc, first_group, gain_ref, o_ref, rows, c * MXU_N)

    @pl.when(i == 0)
    def _():
        stage_rows(h_a, rs_a)
        o_ref[...] = head_ref[...]

    @pl.when((i > 0) & (i % 2 == 1))
    def _():
        stage_rows(h_b, rs_b)
        project(h_a, rs_a)

    @pl.when((i > 0) & (i % 2 == 0))
    def _():
        stage_rows(h_a, rs_a)
        project(h_b, rs_b)


def _inproj(x, g1, w_in16, gain, proj_head):
    assert _IN_SUBS <= _IN_NJ
    x_index = lambda i, j: (
        jnp.minimum(i + 1, _IN_NI - 1) * _IN_SUBS + jnp.minimum(j, _IN_SUBS - 1), 0)
    pinned = lambda i, j: jnp.where(i == 0, 0, j)
    return pl.pallas_call(
        _inproj_kernel,
        name="inproj",
        grid=(_IN_NI, _IN_NJ),
        in_specs=[
            pl.BlockSpec((_IN_SUB_ROWS, D_MODEL), x_index),
            pl.BlockSpec((1, D_MODEL), lambda i, j: (0, 0)),
            pl.BlockSpec((D_MODEL, INPROJ_BN), lambda i, j: (0, _w_col(pinned(i, j), INPROJ_BN))),
            pl.BlockSpec((1, INPROJ_BN), lambda i, j: (0, pinned(i, j))),
            pl.BlockSpec((INPROJ_BM, INPROJ_BN), lambda i, j: (0, jnp.where(i == 0, j, _IN_NJ - 1))),
        ],
        out_specs=pl.BlockSpec((INPROJ_BM, INPROJ_BN), lambda i, j: (i, j)),
        out_shape=jax.ShapeDtypeStruct((SEQ, IN_COLS), BF16),
        scratch_shapes=[
            pltpu.VMEM((INPROJ_BM, D_MODEL), BF16), pltpu.VMEM((INPROJ_BM, 1), F32),
            pltpu.VMEM((INPROJ_BM, D_MODEL), BF16), pltpu.VMEM((INPROJ_BM, 1), F32),
        ],
        compiler_params=_params("arbitrary", "arbitrary"),
    )(x, g1, w_in16, gain, proj_head)


MIX_ROWS = 2 * BLOCK
_MIX_STEPS = SEQ // MIX_ROWS
_MIX_EXT = MIX_ROWS + 2 * POOL_HALO
_WO_SLAB = D_MODEL // _MIX_STEPS
KEYS = 3 * BLOCK
_P_COLBLOCK = ATTN_WIDTH // POOL_WIDTH
_K_COLBLOCK = (ATTN_WIDTH + POOL_WIDTH) // KV_WIDTH
_V_COLBLOCK = _K_COLBLOCK + 1


def _mixers_kernel(sink_ref, q_ref, kp_ref, kc_ref, kn_ref, vp_ref, vc_ref, vn_ref,
                   pp_ref, pc_ref, pn_ref, pw_ref, sc_ref, wo_ref, o_ref, wo16_ref,
                   bias_ref, band_ref):
    n = pl.program_id(0)
    wo16_ref[...] = wo_ref[...].astype(BF16)

    @pl.when(n == 0)
    def _():
        qi = lax.broadcasted_iota(jnp.int32, (BLOCK, KEYS), 0)
        kj = lax.broadcasted_iota(jnp.int32, (BLOCK, KEYS), 1)
        dist = jnp.abs(kj - BLOCK - qi)
        in_band = dist <= WINDOW
        distf = dist.astype(F32)
        for v, valid in enumerate((in_band, in_band & (kj >= BLOCK), in_band & (kj < 2 * BLOCK))):
            for h in range(N_HEADS):
                bias_ref[v * N_HEADS + h] = jnp.where(
                    valid, (-ALIBI_SLOPES[h] * LOG2E) * distf, NEG_INF)

        t = lax.broadcasted_iota(jnp.int32, (MIX_ROWS, _MIX_EXT), 0)
        s_rel = lax.broadcasted_iota(jnp.int32, (MIX_ROWS, _MIX_EXT), 1) - POOL_HALO
        for gi, w in enumerate(POOL_WINDOWS):
            left = w // 2
            right = w - 1 - left
            band_ref[gi] = jnp.where((s_rel >= t - left) & (s_rel <= t + right), 1.0, 0.0).astype(BF16)

    is_first = n == 0
    is_last = n == _MIX_STEPS - 1
    row_head = lax.broadcasted_iota(jnp.int32, (GROUP * BLOCK, 1), 0) // BLOCK
    ones_kd = jnp.ones((KEYS, HEAD_DIM), BF16)

    p_cur = pc_ref[...]
    p_prev = jnp.where(is_first, jnp.zeros_like(pp_ref), pp_ref[...])
    p_next = jnp.where(is_last, jnp.zeros_like(pn_ref), pn_ref[...])
    p_ext = jnp.concatenate([p_prev, p_cur, p_next], axis=0)
    tg = n * MIX_ROWS + lax.broadcasted_iota(jnp.int32, (MIX_ROWS, 1), 0)

    def pool_group(gi):
        w = POOL_WINDOWS[gi]
        left = w // 2
        right = w - 1 - left
        cs = slice(gi * POOL_GROUP_W, (gi + 1) * POOL_GROUP_W)
        win_sum = jnp.dot(band_ref[gi], p_ext[:, cs], preferred_element_type=F32)
        cnt = jnp.minimum(tg + right + 1, SEQ) - jnp.maximum(tg - left, 0)
        u = win_sum * (1.0 / cnt.astype(F32)) - p_cur[:, cs].astype(F32)
        y = jnp.dot(u.astype(BF16), pw_ref[gi], preferred_element_type=F32)
        o_ref[:, ATTN_WIDTH + gi * POOL_GROUP_W:ATTN_WIDTH + (gi + 1) * POOL_GROUP_W] = (
            y * sc_ref[:, cs]).astype(BF16)

    assert N_POOL_GROUPS == N_KV_HEADS
    for kh in range(N_KV_HEADS):
        ksl = slice(kh * HEAD_DIM, (kh + 1) * HEAD_DIM)
        k_all = jnp.concatenate([kp_ref[:, ksl], kc_ref[:, ksl], kn_ref[:, ksl]], axis=0)
        v_all = jnp.concatenate([vp_ref[:, ksl], vc_ref[:, ksl], vn_ref[:, ksl]], axis=0)
        h0 = kh * GROUP
        sink = jnp.full((GROUP * BLOCK, 1), sink_ref[h0 + GROUP - 1] * LOG2E, F32)
        for g in range(GROUP - 1):
            sink = jnp.where(row_head == g, sink_ref[h0 + g] * LOG2E, sink)
        for qb in range(MIX_ROWS // BLOCK):
            qrows = slice(qb * BLOCK, (qb + 1) * BLOCK)
            variant = jnp.where(is_first, 1, 0) if qb == 0 else jnp.where(is_last, 2, 0)
            kw = k_all[qb * BLOCK:qb * BLOCK + KEYS]
            vw = v_all[qb * BLOCK:qb * BLOCK + KEYS]
            qs = jnp.concatenate(
                [q_ref[qrows, (h0 + g) * HEAD_DIM:(h0 + g + 1) * HEAD_DIM] for g in range(GROUP)],
                axis=0)
            s = lax.dot_general(qs, kw, (((1,), (1,)), ((), ())), preferred_element_type=F32)
            bias = bias_ref[pl.ds(variant * N_HEADS + h0, GROUP)].reshape(GROUP * BLOCK, KEYS)
            logits = s + bias
            m = jnp.maximum(jnp.max(logits, axis=-1, keepdims=True), sink)
            e = jnp.exp2(logits - m)
            ov = jnp.dot(e.astype(BF16), jnp.concatenate([vw, ones_kd], axis=1),
                         preferred_element_type=F32)
            o = ov[:, :HEAD_DIM] * (1.0 / (ov[:, HEAD_DIM:] + jnp.exp2(sink - m)))
            for g in range(GROUP):
                o_ref[qrows, (h0 + g) * HEAD_DIM:(h0 + g + 1) * HEAD_DIM] = (
                    o[g * BLOCK:(g + 1) * BLOCK].astype(BF16))
        pool_group(kh)


def _mixers(sink_logits, proj, pool_w16, pool_scale, w_out):
    last_qblock = SEQ // BLOCK - 1
    last_halo = SEQ // POOL_HALO - 1
    qb_per_step = MIX_ROWS // BLOCK
    halo_per_step = MIX_ROWS // POOL_HALO
    kv_prev = lambda col: pl.BlockSpec(
        (BLOCK, KV_WIDTH), lambda n: (jnp.maximum(n * qb_per_step - 1, 0), col))
    kv_cur = lambda col: pl.BlockSpec((MIX_ROWS, KV_WIDTH), lambda n: (n, col))
    kv_next = lambda col: pl.BlockSpec(
        (BLOCK, KV_WIDTH), lambda n: (jnp.minimum((n + 1) * qb_per_step, last_qblock), col))
    return pl.pallas_call(
        _mixers_kernel,
        name="mixers",
        grid=(_MIX_STEPS,),
        in_specs=[
            pl.BlockSpec(memory_space=pltpu.SMEM),
            pl.BlockSpec((MIX_ROWS, ATTN_WIDTH), lambda n: (n, 0)),
            kv_prev(_K_COLBLOCK), kv_cur(_K_COLBLOCK), kv_next(_K_COLBLOCK),
            kv_prev(_V_COLBLOCK), kv_cur(_V_COLBLOCK), kv_next(_V_COLBLOCK),
            pl.BlockSpec((POOL_HALO, POOL_WIDTH),
                         lambda n: (jnp.maximum(n * halo_per_step - 1, 0), _P_COLBLOCK)),
            pl.BlockSpec((MIX_ROWS, POOL_WIDTH), lambda n: (n, _P_COLBLOCK)),
            pl.BlockSpec((POOL_HALO, POOL_WIDTH),
                         lambda n: (jnp.minimum((n + 1) * halo_per_step, last_halo), _P_COLBLOCK)),
            pl.BlockSpec((N_POOL_GROUPS, POOL_GROUP_W, POOL_GROUP_W), lambda n: (0, 0, 0)),
            pl.BlockSpec((1, POOL_WIDTH), lambda n: (0, 0)),
            pl.BlockSpec((_WO_SLAB, D_MODEL), lambda n: (n, 0)),
        ],
        out_specs=[
            pl.BlockSpec((MIX_ROWS, D_MODEL), lambda n: (n, 0)),
            pl.BlockSpec((_WO_SLAB, D_MODEL), lambda n: (n, 0)),
        ],
        out_shape=[
            jax.ShapeDtypeStruct((SEQ, D_MODEL), BF16),
            jax.ShapeDtypeStruct((D_MODEL, D_MODEL), BF16),
        ],
        scratch_shapes=[
            pltpu.VMEM((3 * N_HEADS, BLOCK, KEYS), F32),
            pltpu.VMEM((N_POOL_GROUPS, MIX_ROWS, _MIX_EXT), BF16),
        ],
        compiler_params=_params("arbitrary"),
    )(sink_logits, proj, proj, proj, proj, proj, proj, proj, proj, proj, proj, pool_w16, pool_scale,
      w_out)


OUTPROJ_BM = 1024
OUTPROJ_BN = 1024
_OUTPROJ_NJ = D_MODEL // OUTPROJ_BN
_OUT_KCHUNK = D_MODEL // _OUTPROJ_NJ
_OUT_MROWS = 512


def _outproj_kernel(mx_hbm, w_ref, x_ref, g2_ref, x1_ref, xg_ref, rs_ref, ss_ref, mx_buf, sems):
    i = pl.program_id(0)
    j = pl.program_id(1)

    def chunk_copy(row_block, c):
        return pltpu.make_async_copy(
            mx_hbm.at[pl.ds(row_block * OUTPROJ_BM, OUTPROJ_BM), pl.ds(c * _OUT_KCHUNK, _OUT_KCHUNK)],
            mx_buf.at[row_block % 2, :, pl.ds(c * _OUT_KCHUNK, _OUT_KCHUNK)], sems.at[c])

    @pl.when((i == 0) & (j == 0))
    def _():
        for c in range(_OUTPROJ_NJ):
            chunk_copy(0, c).start()

    @pl.when(j == 0)
    def _():
        for c in range(_OUTPROJ_NJ):
            chunk_copy(i, c).wait()
        ss_ref[...] = jnp.zeros_like(ss_ref)

    for c in range(_OUTPROJ_NJ):
        @pl.when((j == c) & (i < pl.num_programs(0) - 1))
        def _():
            chunk_copy(i + 1, c).start()

    mx_ref = mx_buf.at[i % 2]

    for r in range(OUTPROJ_BM // _OUT_MROWS):
        rows = slice(r * _OUT_MROWS, (r + 1) * _OUT_MROWS)
        for c in range(OUTPROJ_BN // MXU_N):
            cs = slice(c * MXU_N, (c + 1) * MXU_N)
            x1 = x_ref[rows, cs] + jnp.dot(mx_ref[rows, :], w_ref[:, cs],
                                           preferred_element_type=F32)
            x1_ref[rows, cs] = x1
            xg_ref[rows, cs] = (x1 * g2_ref[:, cs]).astype(BF16)
            sq = x1 * x1
            ss_ref[rows, :] += sum(sq[:, k * LANES:(k + 1) * LANES] for k in range(MXU_N // LANES))

    @pl.when(j == _OUTPROJ_NJ - 1)
    def _():
        ms = jnp.sum(ss_ref[...], axis=-1, keepdims=True) * (1.0 / D_MODEL)
        rs_ref[...] = lax.rsqrt(ms + RMS_EPS)


def _outproj(mixed, w_out16, x, g2):
    return pl.pallas_call(
        _outproj_kernel,
        name="outproj",
        grid=(SEQ // OUTPROJ_BM, _OUTPROJ_NJ),
        in_specs=[
            pl.BlockSpec(memory_space=pl.ANY),
            pl.BlockSpec((D_MODEL, OUTPROJ_BN), lambda i, j: (0, j)),
            pl.BlockSpec((OUTPROJ_BM, OUTPROJ_BN), lambda i, j: (i, j)),
            pl.BlockSpec((1, OUTPROJ_BN), lambda i, j: (0, j)),
        ],
        out_specs=[
            pl.BlockSpec((OUTPROJ_BM, OUTPROJ_BN), lambda i, j: (i, j)),
            pl.BlockSpec((OUTPROJ_BM, OUTPROJ_BN), lambda i, j: (i, j)),
            pl.BlockSpec((OUTPROJ_BM, 1), lambda i, j: (i, 0)),
        ],
        out_shape=[
            jax.ShapeDtypeStruct((SEQ, D_MODEL), F32),
            jax.ShapeDtypeStruct((SEQ, D_MODEL), BF16),
            jax.ShapeDtypeStruct((SEQ, 1), F32),
        ],
        scratch_shapes=[
            pltpu.VMEM((OUTPROJ_BM, LANES), F32),
            pltpu.VMEM((2, OUTPROJ_BM, D_MODEL), BF16),
            pltpu.SemaphoreType.DMA((_OUTPROJ_NJ,)),
        ],
        compiler_params=_params("arbitrary", "arbitrary"),
    )(mixed, w_out16, x, g2)


UP_BM = 2048
_UP_MROWS = 1024
FFN_BF = 256
_N_FF_BLOCKS = D_FF // FFN_BF
DOWN_BM = 512
DOWN_BN = 512
_DOWN_K0 = (0, 2816, 5632, 8448)
_DOWN_KLEN = (2816, 2816, 2816, 2560)
_DOWN_KCHUNKS = len(_DOWN_K0)


def _ffn_up_kernel(xg_ref, rs_ref, wg_ref, wu_ref, wd_ref, a_ref, wd16_ref):
    wg = wg_ref[...].astype(BF16)
    wu = wu_ref[...].astype(BF16)
    for r in range(UP_BM // _UP_MROWS):
        rows = slice(r * _UP_MROWS, (r + 1) * _UP_MROWS)
        xg = xg_ref[rows, :]
        rs = rs_ref[rows, :]
        g = jnp.dot(xg, wg, preferred_element_type=F32) * rs
        u = jnp.dot(xg, wu, preferred_element_type=F32) * rs
        a_ref[rows, :] = (g * jax.nn.sigmoid(g) * u).astype(BF16)

    @pl.when(pl.program_id(0) == 0)
    def _():
        wd16_ref[...] = wd_ref[...].astype(BF16)


def _ffn_up(xg, rs2, w_gate, w_up, w_down):
    wd_index = lambda i, f: (jnp.where(i == 0, f, _N_FF_BLOCKS - 1), 0)
    return pl.pallas_call(
        _ffn_up_kernel,
        name="ffn_up",
        grid=(SEQ // UP_BM, _N_FF_BLOCKS),
        in_specs=[
            pl.BlockSpec((UP_BM, D_MODEL), lambda i, f: (i, 0), pipeline_mode=pl.Buffered(1)),
            pl.BlockSpec((UP_BM, 1), lambda i, f: (i, 0), pipeline_mode=pl.Buffered(1)),
            pl.BlockSpec((D_MODEL, FFN_BF), lambda i, f: (0, f)),
            pl.BlockSpec((D_MODEL, FFN_BF), lambda i, f: (0, f)),
            pl.BlockSpec((FFN_BF, D_MODEL), wd_index),
        ],
        out_specs=[
            pl.BlockSpec((UP_BM, FFN_BF), lambda i, f: (i, f)),
            pl.BlockSpec((FFN_BF, D_MODEL), wd_index),
        ],
        out_shape=[
            jax.ShapeDtypeStruct((SEQ, D_FF), BF16),
            jax.ShapeDtypeStruct((D_FF, D_MODEL), BF16),
        ],
        compiler_params=_params("arbitrary", "arbitrary"),
    )(xg, rs2, w_gate, w_up, w_down)


def _ffn_down_kernel(a_hbm, w_ref, x1_ref, o_ref, a_buf, sems):
    i = pl.program_id(0)
    j = pl.program_id(1)
    n_i = pl.num_programs(0)
    n_j = pl.num_programs(1)

    def chunk_copy(row_block, c):
        k0, kl = _DOWN_K0[c], _DOWN_KLEN[c]
        return pltpu.make_async_copy(
            a_hbm.at[pl.ds(row_block * DOWN_BM, DOWN_BM), pl.ds(k0, kl)],
            a_buf.at[row_block % 2, :, pl.ds(k0, kl)], sems.at[c])

    @pl.when((i == 0) & (j == 0))
    def _():
        for c in range(_DOWN_KCHUNKS):
            chunk_copy(0, c).start()

    @pl.when(j == 0)
    def _():
        for c in range(_DOWN_KCHUNKS):
            chunk_copy(i, c).wait()

    for c in range(_DOWN_KCHUNKS):
        @pl.when((j == n_j - _DOWN_KCHUNKS + c) & (i < n_i - 1))
        def _():
            chunk_copy(i + 1, c).start()

    a_blk = a_buf.at[i % 2]
    for c in range(DOWN_BN // MXU_N):
        cs = slice(c * MXU_N, (c + 1) * MXU_N)
        o_ref[:, cs] = x1_ref[:, cs] + jnp.dot(a_blk[...], w_ref[:, cs], preferred_element_type=F32)


def _ffn_down(a, w_down16, x1):
    assert _DOWN_KCHUNKS <= D_MODEL // DOWN_BN
    return pl.pallas_call(
        _ffn_down_kernel,
        name="ffn_down",
        grid=(SEQ // DOWN_BM, D_MODEL // DOWN_BN),
        in_specs=[
            pl.BlockSpec(memory_space=pl.ANY),
            pl.BlockSpec((D_FF, DOWN_BN), lambda i, j: (0, j)),
            pl.BlockSpec((DOWN_BM, DOWN_BN), lambda i, j: (i, j)),
        ],
        out_specs=pl.BlockSpec((DOWN_BM, DOWN_BN), lambda i, j: (i, j)),
        out_shape=jax.ShapeDtypeStruct((SEQ, D_MODEL), F32),
        scratch_shapes=[
            pltpu.VMEM((2, DOWN_BM, D_FF), BF16),
            pltpu.SemaphoreType.DMA((_DOWN_KCHUNKS,)),
        ],
        compiler_params=_params("arbitrary", "arbitrary"),
    )(a, w_down16, x1)


def kernel(x, norm1_g, w_in, q_norm_g, k_norm_g, sink_logits, pool_w, pool_scale, w_out, norm2_g,
           w_gate, w_up, w_down):
    b, s, d = x.shape
    assert (b, s, d) == (1, SEQ, D_MODEL)
    x2 = x.reshape(s, d)
    gain = jnp.concatenate([
        jnp.tile(q_norm_g, N_HEADS) * (Q_SCALE * LOG2E),
        jnp.ones((POOL_WIDTH,), F32),
        jnp.tile(k_norm_g, N_KV_HEADS),
        jnp.ones((KV_WIDTH,), F32),
    ]).reshape(1, IN_COLS)
    g1 = norm1_g.reshape(1, D_MODEL)
    proj_head, w_in16 = _inproj_head(x2, g1, w_in, gain)
    proj = _inproj(x2, g1, w_in16, gain, proj_head)
    mixed, w_out16 = _mixers(sink_logits, proj, pool_w.astype(BF16),
                             pool_scale.reshape(1, POOL_WIDTH), w_out)
    x1, xg, rs2 = _outproj(mixed, w_out16, x2, norm2_g.reshape(1, D_MODEL))
    a, w_down16 = _ffn_up(xg, rs2, w_gate, w_up, w_down)
    out = _ffn_down(a, w_down16, x1)
    return out.reshape(b, s, d)
```
